```python
import math
import jax, jax.numpy as jnp
from jax import lax
import numpy as np

D_MODEL = 1024
BATCH = 1
SEQ = 16384
DEPTH = 1
DEC_BATCH = 32
DEC_SEQ = 4
PAST_LEN = 16384
PAGE_SIZE = 128

ATT_HEADS = 8
QK_DIM = 64
V_DIM = 2 * QK_DIM
ATT_WIDTH = ATT_HEADS * V_DIM
SM_SCALE = QK_DIM ** -0.5
ROPE_THETA = 10000.0
Q_BLOCK = 128
SSM_WIDTH = D_MODEL
SSM_HEAD_DIM = 64
SSM_HEADS = SSM_WIDTH // SSM_HEAD_DIM
SSM_GROUPS = 2
D_STATE = 128
CONV_W = 4
CONV_DIM = SSM_WIDTH + 2 * SSM_GROUPS * D_STATE
CHUNK = 128
MIX_WIDTH = ATT_WIDTH + SSM_WIDTH
PLE_DIM = 256
EPS = 1e-6
Q_COLS = ATT_HEADS * 2 * QK_DIM
K_COLS = ATT_HEADS * 2 * QK_DIM
V_COLS = ATT_WIDTH
G_COLS = ATT_WIDTH
Z_COLS = SSM_WIDTH
XBC_COLS = CONV_DIM
DT_COLS = SSM_HEADS
IN_COLS = Q_COLS + K_COLS + V_COLS + G_COLS + Z_COLS + XBC_COLS + DT_COLS
IN_SPLITS = [Q_COLS, Q_COLS + K_COLS, Q_COLS + K_COLS + V_COLS,
             Q_COLS + K_COLS + V_COLS + G_COLS,
             Q_COLS + K_COLS + V_COLS + G_COLS + Z_COLS,
             Q_COLS + K_COLS + V_COLS + G_COLS + Z_COLS + XBC_COLS]

kernel_name = "hymba_diffattn_mamba2_decode_step"


def _rmsnorm(x, g):
    xf = x.astype(jnp.float32)
    y = xf * lax.rsqrt(jnp.mean(xf * xf, axis=-1, keepdims=True) + EPS) * g.astype(jnp.float32)
    return y.astype(x.dtype)


def _rope(x, pos):
    d = x.shape[-1]
    inv = ROPE_THETA ** (-jnp.arange(0, d, 2, dtype=jnp.float32) / d)
    ang = pos.astype(jnp.float32)[:, None] * inv[None, :]
    cos = jnp.cos(ang)[:, None, None, :]
    sin = jnp.sin(ang)[:, None, None, :]
    xf = x.astype(jnp.float32)
    x1, x2 = xf[..., : d // 2], xf[..., d // 2:]
    return jnp.concatenate([x1 * cos - x2 * sin, x2 * cos + x1 * sin], axis=-1).astype(x.dtype)


def _diff_core(q, k, v, q_pos, k_pos, lam):
    s = jnp.einsum('bqhmd,bkhmd->bhmqk', q, k, preferred_element_type=jnp.float32) * SM_SCALE
    mask = k_pos[None, :] <= q_pos[:, None]
    s = jnp.where(mask, s, -jnp.inf)
    pr = jax.nn.softmax(s, axis=-1)
    a = pr[:, :, 0] - lam * pr[:, :, 1]
    o = jnp.einsum('bhqk,bkhv->bqhv', a, v.astype(jnp.float32))
    return o.astype(q.dtype)


def _prompt_attend(q, k, v, lam):
    b, L = q.shape[0], q.shape[1]
    qb = min(Q_BLOCK, L)
    nb = L // qb
    k_pos = jnp.arange(L)
    qblocks = jnp.swapaxes(q.reshape(b, nb, qb, ATT_HEADS, 2, QK_DIM), 0, 1)
    starts = jnp.arange(nb) * qb

    def blk(args):
        qi, s0 = args
        return _diff_core(qi, k, v, s0 + jnp.arange(qb), k_pos, lam)

    o = lax.map(blk, (qblocks, starts))
    return jnp.swapaxes(o, 0, 1).reshape(b, L, ATT_HEADS, V_DIM)


def _sample_attend(q, k, v, lam, cache_k, cache_v, page_table, layer):
    L = q.shape[1]
    past = page_table.shape[1] * cache_k.shape[2]
    q_pos = past + jnp.arange(L)
    k_pos = jnp.arange(past + L)

    def one(args):
        pt, qi, ki, vi = args
        kp = cache_k[layer, pt].reshape(past, ATT_HEADS, 2, QK_DIM)
        vp = cache_v[layer, pt].reshape(past, ATT_HEADS, V_DIM)
        kall = jnp.concatenate([kp, ki.astype(kp.dtype)], axis=0)[None]
        vall = jnp.concatenate([vp, vi.astype(vp.dtype)], axis=0)[None]
        return _diff_core(qi[None], kall, vall, q_pos, k_pos, lam)[0]

    return lax.map(one, (page_table, q, k, v))


def _ssd(x, dt, a, bm, cm, h0):
    b, L, H, P = x.shape
    q = min(CHUNK, L)
    nc = L // q
    rep = H // SSM_GROUPS
    xc = x.astype(jnp.float32).reshape(b, nc, q, H, P)
    bh = jnp.repeat(bm.astype(jnp.float32), rep, axis=2).reshape(b, nc, q, H, D_STATE)
    ch = jnp.repeat(cm.astype(jnp.float32), rep, axis=2).reshape(b, nc, q, H, D_STATE)
    dtc = dt.reshape(b, nc, q, H)
    cs = jnp.cumsum(dtc * a, axis=2)
    causal = jnp.tril(jnp.ones((q, q), dtype=bool))
    diff = cs[:, :, :, None, :] - cs[:, :, None, :, :]
    lmat = jnp.exp(jnp.where(causal[None, None, :, :, None], diff, -jnp.inf))
    cb = jnp.einsum('bcthn,bcshn->bctsh', ch, bh)
    w = cb * lmat * dtc[:, :, None, :, :]
    y_diag = jnp.einsum('bctsh,bcshp->bcthp', w, xc)
    decay_end = jnp.exp(cs[:, :, -1:, :] - cs) * dtc
    states = jnp.einsum('bcshn,bcshp->bchpn', bh, xc * decay_end[..., None])
    chunk_decay = jnp.exp(cs[:, :, -1, :])

    def step(h, inp):
        dc, st = inp
        return dc[:, :, None, None] * h + st, h

    h_final, h_prev = lax.scan(step, h0.astype(jnp.float32),
                               (jnp.moveaxis(chunk_decay, 1, 0), jnp.moveaxis(states, 1, 0)))
    h_prev = jnp.moveaxis(h_prev, 0, 1)
    y_off = jnp.einsum('bcthn,bchpn->bcthp', ch, h_prev) * jnp.exp(cs)[..., None]
    return (y_diag + y_off).reshape(b, L, H, P), h_final


def _mamba2(z, xbc, dt_raw, conv_prev, h0, conv_w, conv_b, dt_bias, a_log, d_skip, gnorm_w):
    b, L, _ = xbc.shape
    xcat = jnp.concatenate([conv_prev.astype(xbc.dtype), xbc], axis=1)
    conv_new = xcat[:, -(CONV_W - 1):]
    u = lax.conv_general_dilated(xcat, conv_w[:, None, :].astype(xcat.dtype), (1,), 'VALID',
                                 dimension_numbers=('NWC', 'WIO', 'NWC'),
                                 feature_group_count=CONV_DIM) + conv_b
    u = jax.nn.silu(u)
    xs, bm, cm = jnp.split(u, [SSM_WIDTH, SSM_WIDTH + SSM_GROUPS * D_STATE], axis=-1)
    xs = xs.reshape(b, L, SSM_HEADS, SSM_HEAD_DIM)
    bm = bm.reshape(b, L, SSM_GROUPS, D_STATE)
    cm = cm.reshape(b, L, SSM_GROUPS, D_STATE)
    dt = jax.nn.softplus(dt_raw.astype(jnp.float32) + dt_bias.astype(jnp.float32))
    a = -jnp.exp(a_log.astype(jnp.float32))
    y, h_new = _ssd(xs, dt, a, bm, cm, h0)
    y = y + d_skip.astype(jnp.float32)[:, None] * xs.astype(jnp.float32)
    y = y.reshape(b, L, SSM_WIDTH) * jax.nn.silu(z.astype(jnp.float32))
    yg = y.reshape(b, L, SSM_GROUPS, SSM_WIDTH // SSM_GROUPS)
    yg = yg * lax.rsqrt(jnp.mean(yg * yg, axis=-1, keepdims=True) + EPS)
    y = yg.reshape(b, L, SSM_WIDTH) * gnorm_w.astype(jnp.float32)
    return y.astype(z.dtype), conv_new, h_new.astype(z.dtype)


def _layer(x, p, pos, attend, conv_prev, h0, lam_init, w_norm, w_in, lq1, lk1, lq2, lk2,
           subln_w, conv_w, conv_b, dt_bias, a_log, d_skip, gnorm_w, w_out, w_ple_proj, w_ple_gate):
    b, L, _ = x.shape
    hn = _rmsnorm(x, w_norm)
    proj = hn @ w_in
    q, k, v, g, z, xbc, dt_raw = jnp.split(proj, IN_SPLITS, axis=-1)
    q = _rope(q.reshape(b, L, ATT_HEADS, 2, QK_DIM), pos)
    k = _rope(k.reshape(b, L, ATT_HEADS, 2, QK_DIM), pos)
    v = v.reshape(b, L, ATT_HEADS, V_DIM)
    lam = (jnp.exp(jnp.sum(lq1.astype(jnp.float32) * lk1.astype(jnp.float32)))
           - jnp.exp(jnp.sum(lq2.astype(jnp.float32) * lk2.astype(jnp.float32))) + lam_init)
    o = attend(q, k, v, lam)
    o = _rmsnorm(o, subln_w) * (1.0 - lam_init)
    att_out = o.reshape(b, L, ATT_WIDTH) * jax.nn.silu(g)
    ssm_out, conv_new, h_new = _mamba2(z, xbc, dt_raw, conv_prev, h0, conv_w, conv_b,
                                       dt_bias, a_log, d_skip, gnorm_w)
    hmid = x + jnp.concatenate([att_out, ssm_out], axis=-1) @ w_out
    out = hmid + (p @ w_ple_proj) * jax.nn.sigmoid(hmid @ w_ple_gate)
    return out, k, v, conv_new, h_new


def setup_inputs(seed: int = 0) -> dict:
    key = jax.random.key(seed)
    ks = jax.random.split(key, 32)
    n_pages = PAST_LEN // PAGE_SIZE
    n_pool = (5 * DEC_BATCH * n_pages) // 4
    f32 = jnp.float32
    nrm = lambda k, s, sc: (jax.random.normal(k, s, f32) * sc)
    x_prompt = nrm(ks[0], (BATCH, SEQ, D_MODEL), 1.0)
    x_sample = nrm(ks[1], (DEC_BATCH, DEC_SEQ, D_MODEL), 1.0)
    cache_k = nrm(ks[2], (DEPTH, n_pool, PAGE_SIZE, ATT_HEADS, 2, QK_DIM), 1.0)
    cache_v = nrm(ks[3], (DEPTH, n_pool, PAGE_SIZE, ATT_HEADS, V_DIM), 1.0)
    state_conv = nrm(ks[4], (DEPTH, DEC_BATCH, CONV_W - 1, CONV_DIM), 1.0)
    state_ssm = nrm(ks[5], (DEPTH, DEC_BATCH, SSM_HEADS, SSM_HEAD_DIM, D_STATE), 0.1)
    page_table = jax.random.permutation(ks[6], n_pool)[: DEC_BATCH * n_pages].reshape(
        DEC_BATCH, n_pages).astype(jnp.int32)
    p_prompt = nrm(ks[7], (DEPTH, BATCH, SEQ, PLE_DIM), 1.0)
    p_sample = nrm(ks[8], (DEPTH, DEC_BATCH, DEC_SEQ, PLE_DIM), 1.0)
    w_norm = 1.0 + nrm(ks[9], (DEPTH, D_MODEL), 0.02)
    w_in = nrm(ks[10], (DEPTH, D_MODEL, IN_COLS), D_MODEL ** -0.5)
    lambda_q1 = nrm(ks[11], (DEPTH, QK_DIM), 0.1)
    lambda_k1 = nrm(ks[12], (DEPTH, QK_DIM), 0.1)
    lambda_q2 = nrm(ks[13], (DEPTH, QK_DIM), 0.1)
    lambda_k2 = nrm(ks[14], (DEPTH, QK_DIM), 0.1)
    subln_w = 1.0 + nrm(ks[15], (DEPTH, V_DIM), 0.02)
    conv_w = nrm(ks[16], (DEPTH, CONV_W, CONV_DIM), CONV_W ** -0.5)
    conv_b = nrm(ks[17], (DEPTH, CONV_DIM), 0.02)
    dt0 = jnp.exp(jax.random.uniform(ks[18], (DEPTH, SSM_HEADS), f32,
                                     math.log(1e-3), math.log(1e-1)))
    dt_bias = dt0 + jnp.log(-jnp.expm1(-dt0))
    A_log = jnp.log(jax.random.uniform(ks[19], (DEPTH, SSM_HEADS), f32, 1.0, 16.0))
    D_skip = 1.0 + nrm(ks[20], (DEPTH, SSM_HEADS), 0.02)
    gnorm_w = 1.0 + nrm(ks[21], (DEPTH, SSM_WIDTH), 0.02)
    w_out = nrm(ks[22], (DEPTH, MIX_WIDTH, D_MODEL), MIX_WIDTH ** -0.5)
    w_ple_proj = nrm(ks[23], (DEPTH, PLE_DIM, D_MODEL), PLE_DIM ** -0.5)
    w_ple_gate = nrm(ks[24], (DEPTH, D_MODEL, D_MODEL), D_MODEL ** -0.5)
    final_norm_w = 1.0 + nrm(ks[25], (D_MODEL,), 0.02)
    return {"x_prompt": x_prompt, "x_sample": x_sample, "cache_k": cache_k, "cache_v": cache_v,
            "state_conv": state_conv, "state_ssm": state_ssm, "page_table": page_table,
            "p_prompt": p_prompt, "p_sample": p_sample, "w_norm": w_norm, "w_in": w_in,
            "lambda_q1": lambda_q1, "lambda_k1": lambda_k1, "lambda_q2": lambda_q2,
            "lambda_k2": lambda_k2, "subln_w": subln_w, "conv_w": conv_w, "conv_b": conv_b,
            "dt_bias": dt_bias, "A_log": A_log, "D_skip": D_skip, "gnorm_w": gnorm_w,
            "w_out": w_out, "w_ple_proj": w_ple_proj, "w_ple_gate": w_ple_gate,
            "final_norm_w": final_norm_w}


def reference(x_prompt, x_sample, cache_k, cache_v, state_conv, state_ssm, page_table,
              p_prompt, p_sample, w_norm, w_in, lambda_q1, lambda_k1, lambda_q2, lambda_k2,
              subln_w, conv_w, conv_b, dt_bias, A_log, D_skip, gnorm_w, w_out,
              w_ple_proj, w_ple_gate, final_norm_w):
    b_p, seq = x_prompt.shape[0], x_prompt.shape[1]
    dec_seq = x_sample.shape[1]
    past = page_table.shape[1] * cache_k.shape[2]
    pos_prompt = jnp.arange(seq)
    pos_sample = past + jnp.arange(dec_seq)
    conv0 = jnp.zeros((b_p, CONV_W - 1, CONV_DIM), x_prompt.dtype)
    h00 = jnp.zeros((b_p, SSM_HEADS, SSM_HEAD_DIM, D_STATE), jnp.float32)
    hp, hs = x_prompt, x_sample
    kp_l, vp_l, cp_l, sp_l, ks_l, vs_l, cs_l, ss_l = [], [], [], [], [], [], [], []
    for i in range(DEPTH):
        lam_init = 0.8 - 0.6 * math.exp(-0.3 * i)
        wts = (w_norm[i], w_in[i], lambda_q1[i], lambda_k1[i], lambda_q2[i], lambda_k2[i],
               subln_w[i], conv_w[i], conv_b[i], dt_bias[i], A_log[i], D_skip[i], gnorm_w[i],
               w_out[i], w_ple_proj[i], w_ple_gate[i])
        hp, kp, vp, cp, sp = _layer(hp, p_prompt[i], pos_prompt, _prompt_attend, conv0, h00,
                                    lam_init, *wts)
        samp_att = lambda q, k, v, lam, _i=i: _sample_attend(q, k, v, lam, cache_k, cache_v,
                                                             page_table, _i)
        hs, ksn, vsn, csn, ssn = _layer(hs, p_sample[i], pos_sample, samp_att, state_conv[i],
                                        state_ssm[i], lam_init, *wts)
        kp_l.append(kp); vp_l.append(vp); cp_l.append(cp); sp_l.append(sp)
        ks_l.append(ksn); vs_l.append(vsn); cs_l.append(csn); ss_l.append(ssn)
    y_prompt = _rmsnorm(hp, final_norm_w)
    y_sample = _rmsnorm(hs, final_norm_w)
    return (y_prompt, y_sample, jnp.stack(kp_l), jnp.stack(vp_l), jnp.stack(cp_l),
            jnp.stack(sp_l), jnp.stack(ks_l), jnp.stack(vs_l), jnp.stack(cs_l), jnp.stack(ss_l))
```

```python
import functools
import math

import jax
import jax.numpy as jnp
from jax import lax
from jax.experimental import pallas as pl
from jax.experimental.pallas import tpu as pltpu

F32 = jnp.float32
BF16 = jnp.bfloat16

D_MODEL = 1024
ATT_HEADS = 8
QK_DIM = 64
V_DIM = 2 * QK_DIM
ATT_WIDTH = ATT_HEADS * V_DIM
SM_SCALE = QK_DIM ** -0.5
ROPE_THETA = 10000.0
SSM_WIDTH = 1024
SSM_HEAD_DIM = 64
SSM_HEADS = SSM_WIDTH // SSM_HEAD_DIM
SSM_GROUPS = 2
D_STATE = 128
CONV_W = 4
CONV_DIM = SSM_WIDTH + 2 * SSM_GROUPS * D_STATE
CHUNK = 128
PLE_DIM = 256
EPS = 1e-6
LAM_INIT = 0.8 - 0.6 * math.exp(-0.3 * 0)

LANES = 128
SUBLANES = 8
NEG_BIG = -1e30
VMEM_LIMIT = 56 * 1024 * 1024


def _cparams(n_axes):
    return pltpu.CompilerParams(dimension_semantics=("arbitrary",) * n_axes,
                                vmem_limit_bytes=VMEM_LIMIT)


def _const_spec(shape):
    nd = len(shape)
    return pl.BlockSpec(shape, lambda *_: (0,) * nd)


def _silu(x):
    return x * jax.nn.sigmoid(x)


def _inproj_body(x_ref, wn_ref, wq_ref, wk_ref, wv_ref, wg_ref, wz_ref, wx_ref, wdt_ref,
                 cos_ref, sin_ref,
                 q_ref, k_ref, kb_ref, v_ref, vb_ref, g_ref, z_ref, xbc_ref, dt_ref):
    x = x_ref[...]
    hn = x * lax.rsqrt(jnp.mean(x * x, axis=-1, keepdims=True) + EPS) * wn_ref[...]
    hn = hn.astype(BF16)
    cos = cos_ref[...]
    sin = sin_ref[...]
    lane = lax.broadcasted_iota(jnp.int32, cos.shape, 1)
    first_half = (lane % QK_DIM) < (QK_DIM // 2)

    def rope_chunks(t):
        for c in range(t.shape[1] // LANES):
            tc = t[:, c * LANES:(c + 1) * LANES]
            partner = jnp.where(first_half,
                                pltpu.roll(tc, LANES - QK_DIM // 2, 1),
                                pltpu.roll(tc, QK_DIM // 2, 1))
            yield c, tc * cos + partner * sin

    q = jnp.dot(hn, wq_ref[...], preferred_element_type=F32)
    for c, r in rope_chunks(q):
        q_ref[:, c * LANES:(c + 1) * LANES] = (r * SM_SCALE).astype(BF16)
    k = jnp.dot(hn, wk_ref[...], preferred_element_type=F32)
    for c, r in rope_chunks(k):
        k_ref[:, c * LANES:(c + 1) * LANES] = r
        kb_ref[:, c * LANES:(c + 1) * LANES] = r.astype(BF16)
    v = jnp.dot(hn, wv_ref[...], preferred_element_type=F32)
    v_ref[...] = v
    vb_ref[...] = v.astype(BF16)
    g_ref[...] = jnp.dot(hn, wg_ref[...], preferred_element_type=F32)
    z_ref[...] = jnp.dot(hn, wz_ref[...], preferred_element_type=F32)
    xbc_ref[...] = jnp.dot(hn, wx_ref[...], preferred_element_type=F32)
    dt_ref[...] = jnp.dot(hn, wdt_ref[...], preferred_element_type=F32)


def _inproj(x, w, cos, sin, tm):
    rows = x.shape[0]
    grid = (rows // tm,)
    row = lambda n: pl.BlockSpec((tm, n), lambda i: (i, 0))
    wspec = lambda a: pl.BlockSpec(a.shape, lambda i: (0, 0), pipeline_mode=pl.Buffered(1))
    out_shapes = (
        jax.ShapeDtypeStruct((rows, ATT_WIDTH), BF16),
        jax.ShapeDtypeStruct((rows, ATT_WIDTH), F32),
        jax.ShapeDtypeStruct((rows, ATT_WIDTH), BF16),
        jax.ShapeDtypeStruct((rows, ATT_WIDTH), F32),
        jax.ShapeDtypeStruct((rows, ATT_WIDTH), BF16),
        jax.ShapeDtypeStruct((rows, ATT_WIDTH), F32),
        jax.ShapeDtypeStruct((rows, SSM_WIDTH), F32),
        jax.ShapeDtypeStruct((rows, CONV_DIM), F32),
        jax.ShapeDtypeStruct((rows, LANES), F32),
    )
    return pl.pallas_call(
        _inproj_body,
        grid=grid,
        in_specs=[row(D_MODEL), wspec(w["norm"]), wspec(w["q"]), wspec(w["k"]), wspec(w["v"]),
                  wspec(w["g"]), wspec(w["z"]), wspec(w["xbc"]), wspec(w["dt"]),
                  row(LANES), row(LANES)],
        out_specs=[row(s.shape[1]) for s in out_shapes],
        out_shape=out_shapes,
        compiler_params=_cparams(1),
        name="inproj",
    )(x, w["norm"], w["q"], w["k"], w["v"], w["g"], w["z"], w["xbc"], w["dt"], cos, sin)


def _lambda_value(lq1_ref, lk1_ref, lq2_ref, lk2_ref):
    a = jnp.sum(lq1_ref[...] * lk1_ref[...], axis=-1, keepdims=True)
    b = jnp.sum(lq2_ref[...] * lk2_ref[...], axis=-1, keepdims=True)
    return jnp.exp(a) - jnp.exp(b) + LAM_INIT


def _sub_norm_gate(o, sub_w, g):
    o = o * lax.rsqrt(jnp.mean(o * o, axis=-1, keepdims=True) + EPS) * sub_w * (1.0 - LAM_INIT)
    return o * _silu(g)


def _split_maps(q):
    lane = lax.broadcasted_iota(jnp.int32, q.shape, 1)
    qf = q.astype(F32)
    return jnp.concatenate([jnp.where(lane < QK_DIM, qf, 0.0),
                            jnp.where(lane >= QK_DIM, qf, 0.0)], axis=0).astype(BF16)


def _softmax_step(s, vt, m_ref, acc_ref):
    m_prev = m_ref[...]
    m_new = jnp.maximum(m_prev, jnp.max(s, axis=-1, keepdims=True))
    alpha = jnp.exp(m_prev - m_new)
    p = jnp.exp(s - m_new).astype(BF16)
    vext = jnp.concatenate([vt, jnp.ones_like(vt)], axis=1)
    pv = jnp.dot(p, vext, preferred_element_type=F32)
    acc_ref[...] = alpha * acc_ref[...] + pv
    m_ref[...] = m_new


def _prompt_attn_body(q_ref, k_ref, v_ref, g_ref, sub_ref, lq1_ref, lk1_ref, lq2_ref, lk2_ref,
                      o_ref, m_ref, acc_ref, *, tq, tk):
    i = pl.program_id(1)
    qq = _split_maps(q_ref[...])
    m_ref[...] = jnp.full(m_ref.shape, NEG_BIG, F32)
    acc_ref[...] = jnp.zeros(acc_ref.shape, F32)

    def scores(j):
        kt = k_ref[pl.ds(pl.multiple_of(j * tk, tk), tk), :]
        return lax.dot_general(qq, kt, (((1,), (1,)), ((), ())), preferred_element_type=F32)

    def vtile(j):
        return v_ref[pl.ds(pl.multiple_of(j * tk, tk), tk), :]

    def full_step(j, carry):
        _softmax_step(scores(j), vtile(j), m_ref, acc_ref)
        return carry

    n_full = (i * tq) // tk
    lax.fori_loop(0, n_full, full_step, 0)

    s = scores(n_full)
    qpos = i * tq + lax.broadcasted_iota(jnp.int32, s.shape, 0) % tq
    kpos = n_full * tk + lax.broadcasted_iota(jnp.int32, s.shape, 1)
    _softmax_step(jnp.where(kpos <= qpos, s, NEG_BIG), vtile(n_full), m_ref, acc_ref)

    acc = acc_ref[...]
    o1 = acc[:tq, :V_DIM] / acc[:tq, V_DIM:]
    o2 = acc[tq:, :V_DIM] / acc[tq:, V_DIM:]
    lam = _lambda_value(lq1_ref, lk1_ref, lq2_ref, lk2_ref)
    o_ref[...] = _sub_norm_gate(o1 - lam * o2, sub_ref[...], g_ref[...]).astype(o_ref.dtype)


def _prompt_attn(q, k, v, g, sub_w, lams, tq, tk):
    seq = q.shape[0]
    assert seq % tq == 0 and seq % tk == 0 and tk % tq == 0
    head_tile = pl.BlockSpec((tq, V_DIM), lambda h, i: (i, h))
    head_all = pl.BlockSpec((seq, V_DIM), lambda h, i: (0, h))
    small = lambda a: pl.BlockSpec(a.shape, lambda h, i: (0, 0))
    return pl.pallas_call(
        functools.partial(_prompt_attn_body, tq=tq, tk=tk),
        grid=(ATT_HEADS, seq // tq),
        in_specs=[head_tile, head_all, head_all, head_tile, small(sub_w)] + [small(a) for a in lams],
        out_specs=head_tile,
        out_shape=jax.ShapeDtypeStruct((seq, ATT_WIDTH), BF16),
        scratch_shapes=[pltpu.VMEM((2 * tq, 1), F32), pltpu.VMEM((2 * tq, 2 * V_DIM), F32)],
        compiler_params=_cparams(2),
        name="prompt_attn",
    )(q, k, v, g, sub_w, *lams)


QROWS = 16


def _decode_attn_body(pt_ref, q_ref, kc_ref, vc_ref, kn_ref, vn_ref, g_ref, sub_ref,
                      lq1_ref, lk1_ref, lq2_ref, lk2_ref,
                      o_ref, m_ref, acc_ref, kbuf_ref, vbuf_ref, *, n_pages, dec_seq):
    del pt_ref
    p_idx = pl.program_id(1)

    @pl.when(p_idx == 0)
    def _():
        m_ref[...] = jnp.full(m_ref.shape, NEG_BIG, F32)
        acc_ref[...] = jnp.zeros(acc_ref.shape, F32)

    row = lax.broadcasted_iota(jnp.int32, q_ref.shape, 0)
    lane = lax.broadcasted_iota(jnp.int32, q_ref.shape, 1) % V_DIM
    keep = ((row < dec_seq) & (lane < QK_DIM)) | ((row >= dec_seq) & (lane >= QK_DIM))
    q = jnp.where(keep, q_ref[...].astype(F32), 0.0).astype(BF16)

    def process(score_fn, value_fn, new_tokens):
        s = jnp.concatenate([score_fn(h, q[:, h * V_DIM:(h + 1) * V_DIM])
                             for h in range(ATT_HEADS)], axis=0)
        if new_tokens:
            qi = lax.broadcasted_iota(jnp.int32, s.shape, 0) % dec_seq
            kj = lax.broadcasted_iota(jnp.int32, s.shape, 1)
            s = jnp.where(kj <= qi, s, NEG_BIG)
        m_prev = m_ref[...]
        m_new = jnp.maximum(m_prev, jnp.max(s, axis=-1, keepdims=True))
        alpha = jnp.exp(m_prev - m_new)
        p = jnp.exp(s - m_new).astype(BF16)
        pv_parts = []
        for h in range(ATT_HEADS):
            vh = value_fn(h).astype(BF16)
            vext = jnp.concatenate([vh, jnp.ones_like(vh)], axis=1)
            pv_parts.append(jnp.dot(p[h * QROWS:(h + 1) * QROWS], vext,
                                    preferred_element_type=F32))
        acc_ref[...] = alpha * acc_ref[...] + jnp.concatenate(pv_parts, axis=0)
        m_ref[...] = m_new

    @pl.when(p_idx < n_pages)
    def _():
        def score(h, qh):
            kt = kc_ref[h * V_DIM:(h + 1) * V_DIM, :].astype(BF16)
            return jnp.dot(qh, kt, preferred_element_type=F32)
        process(score, lambda h: vc_ref[:, h, :], False)

    @pl.when(p_idx == n_pages)
    def _():
        kbuf_ref[...] = jnp.zeros(kbuf_ref.shape, F32)
        vbuf_ref[...] = jnp.zeros(vbuf_ref.shape, F32)
        kbuf_ref[0:SUBLANES, :] = kn_ref[...]
        vbuf_ref[0:SUBLANES, :] = vn_ref[...]

        def score(h, qh):
            kh = kbuf_ref[:, h * V_DIM:(h + 1) * V_DIM].astype(BF16)
            return lax.dot_general(qh, kh, (((1,), (1,)), ((), ())), preferred_element_type=F32)
        process(score, lambda h: vbuf_ref[:, h * V_DIM:(h + 1) * V_DIM], True)

        acc = acc_ref[...]
        t = acc[:, :V_DIM] / acc[:, V_DIM:]
        lam = _lambda_value(lq1_ref, lk1_ref, lq2_ref, lk2_ref)
        g = g_ref[...]
        for h in range(ATT_HEADS):
            cols = slice(h * V_DIM, (h + 1) * V_DIM)
            th = t[h * QROWS:h * QROWS + SUBLANES]
            o = th - lam * pltpu.roll(th, SUBLANES - dec_seq, 0)
            o_ref[:, cols] = _sub_norm_gate(o, sub_ref[...], g[:, cols])


def _decode_attn(page_table, q16, cache_kt, cache_v, k_new8, v_new8, g8, sub_w, lams, dec_seq):
    batch, n_pages = page_table.shape
    page = cache_kt.shape[2]
    assert page == LANES and 2 * dec_seq == SUBLANES
    last = n_pages - 1
    per_b = lambda r: pl.BlockSpec((None, r, ATT_WIDTH), lambda b, p, pt: (b, 0, 0))
    page_of = lambda b, p, pt: pt[b, jnp.minimum(p, last)]
    k_paged = pl.BlockSpec((None, ATT_WIDTH, page), lambda b, p, pt: (page_of(b, p, pt), 0, 0))
    v_paged = pl.BlockSpec((None, page, ATT_HEADS, V_DIM),
                           lambda b, p, pt: (page_of(b, p, pt), 0, 0, 0))
    small = lambda a: pl.BlockSpec(a.shape, lambda b, p, pt: (0, 0))
    grid_spec = pltpu.PrefetchScalarGridSpec(
        num_scalar_prefetch=1,
        grid=(batch, n_pages + 1),
        in_specs=[per_b(QROWS), k_paged, v_paged, per_b(SUBLANES), per_b(SUBLANES), per_b(SUBLANES),
                  small(sub_w)] + [small(a) for a in lams],
        out_specs=per_b(SUBLANES),
        scratch_shapes=[pltpu.VMEM((ATT_HEADS * QROWS, 1), F32),
                        pltpu.VMEM((ATT_HEADS * QROWS, 2 * V_DIM), F32),
                        pltpu.VMEM((page, ATT_WIDTH), F32),
                        pltpu.VMEM((page, ATT_WIDTH), F32)],
    )
    return pl.pallas_call(
        functools.partial(_decode_attn_body, n_pages=n_pages, dec_seq=dec_seq),
        grid_spec=grid_spec,
        out_shape=jax.ShapeDtypeStruct((batch, SUBLANES, ATT_WIDTH), F32),
        compiler_params=_cparams(2),
        name="decode_attn",
    )(page_table, q16, cache_kt, cache_v, k_new8, v_new8, g8, sub_w, *lams)


HEADS_PER_GROUP = SSM_HEADS // SSM_GROUPS
GROUP_WIDTH = SSM_WIDTH // SSM_GROUPS
TAIL = SUBLANES


def _pair_expand(a, j):
    lane = lax.broadcasted_iota(jnp.int32, (a.shape[0], LANES), 1)
    return jnp.where(lane < SSM_HEAD_DIM, a[:, 2 * j:2 * j + 1], a[:, 2 * j + 1:2 * j + 2])


def _ssd_body(xbc_ref, dt_ref, z_ref, cprev_ref, h0_ref, cw_ref, cb_ref, dtb_ref, alog_ref,
              dskip_ref, gn_ref,
              y_ref, cnew_ref, hnew_ref,
              xs_ref, dtp_ref, ht_ref, *, valid, n_chunks):
    q = CHUNK
    c = pl.program_id(1)
    first_step = (pl.program_id(0) == 0) & (c == 0)

    @pl.when(first_step)
    def _():
        xs_ref[...] = jnp.zeros(xs_ref.shape, F32)
        dtp_ref[...] = jnp.zeros(dtp_ref.shape, F32)

    @pl.when(c == 0)
    def _():
        xs_ref[TAIL - (CONV_W - 1):TAIL, :] = cprev_ref[...]
        ht_ref[...] = jnp.transpose(h0_ref[...])

    xs_ref[TAIL:TAIL + valid, :] = xbc_ref[...]
    dtp_ref[0:valid, :] = dt_ref[...]

    u = cb_ref[...]
    for w in range(CONV_W):
        u = u + xs_ref[pl.ds(TAIL - (CONV_W - 1) + w, q), :] * cw_ref[w:w + 1, :]
    u = _silu(u)

    @pl.when(c == n_chunks - 1)
    def _():
        cnew_ref[...] = xs_ref[pl.ds(TAIL + valid - (CONV_W - 1), CONV_W - 1), :]

    if n_chunks > 1:
        xs_ref[0:TAIL, :] = xs_ref[q:q + TAIL, :]

    xs = u[:, :SSM_WIDTH]
    bm = u[:, SSM_WIDTH:SSM_WIDTH + SSM_GROUPS * D_STATE]
    cm = u[:, SSM_WIDTH + SSM_GROUPS * D_STATE:]

    row = lax.broadcasted_iota(jnp.int32, (q, LANES), 0)
    x_dt = dtp_ref[...] + dtb_ref[...]
    dt = jnp.maximum(x_dt, 0.0) + jnp.log1p(jnp.exp(-jnp.abs(x_dt)))
    if valid < q:
        dt = jnp.where(row < valid, dt, 0.0)
    a = -jnp.exp(alog_ref[...])
    da = dt * a

    tt = lax.broadcasted_iota(jnp.int32, (q, q), 0)
    ss = lax.broadcasted_iota(jnp.int32, (q, q), 1)
    causal = ss <= tt
    tril = jnp.where(causal, 1.0, 0.0).astype(BF16)
    da_hi = da.astype(BF16)
    r1 = da - da_hi.astype(F32)
    da_mid = r1.astype(BF16)
    da_lo = (r1 - da_mid.astype(F32)).astype(BF16)
    cs3 = jnp.dot(tril, jnp.concatenate([da_hi, da_mid, da_lo], axis=1), preferred_element_type=F32)
    cs = cs3[:, :LANES] + cs3[:, LANES:2 * LANES] + cs3[:, 2 * LANES:]

    cs_t = jnp.transpose(cs)
    dt_t = jnp.transpose(dt)
    cs_last = cs[q - 1:q, :]
    e_cs = jnp.exp(cs)
    dec_end = jnp.exp(cs_last - cs) * dt
    chunk_decay = jnp.exp(cs_last)

    ht = ht_ref[...]
    ht_b = ht.astype(BF16)
    y_parts = []
    st_parts = []
    for grp in range(SSM_GROUPS):
        bg = bm[:, grp * D_STATE:(grp + 1) * D_STATE]
        cg = cm[:, grp * D_STATE:(grp + 1) * D_STATE].astype(BF16)
        cb = lax.dot_general(cg, bg.astype(BF16), (((1,), (1,)), ((), ())),
                             preferred_element_type=F32)
        y_off = jnp.dot(cg, ht_b[:, grp * GROUP_WIDTH:(grp + 1) * GROUP_WIDTH],
                        preferred_element_type=F32)
        xd_parts = []
        for jp in range(HEADS_PER_GROUP // 2):
            j = grp * (HEADS_PER_GROUP // 2) + jp
            x_pair = xs[:, j * LANES:(j + 1) * LANES]
            x_pair_b = x_pair.astype(BF16)
            y_pair = []
            for h in (2 * j, 2 * j + 1):
                diff = cs[:, h:h + 1] - cs_t[h:h + 1, :]
                lmat = jnp.exp(jnp.where(causal, diff, NEG_BIG))
                wmat = (cb * lmat * dt_t[h:h + 1, :]).astype(BF16)
                y_pair.append(jnp.dot(wmat, x_pair_b, preferred_element_type=F32))
            lane = lax.broadcasted_iota(jnp.int32, (q, LANES), 1)
            y_diag = jnp.where(lane < SSM_HEAD_DIM, y_pair[0], y_pair[1])
            y_parts.append(y_diag + y_off[:, jp * LANES:(jp + 1) * LANES] * _pair_expand(e_cs, j))
            xd_parts.append((x_pair * _pair_expand(dec_end, j)).astype(BF16))
        xd = jnp.concatenate(xd_parts, axis=1)
        st_parts.append(jnp.dot(jnp.transpose(bg).astype(BF16), xd, preferred_element_type=F32))
    decay_lanes = jnp.concatenate([_pair_expand(chunk_decay, j) for j in range(SSM_HEADS // 2)], axis=1)
    ht_new = ht * decay_lanes + jnp.concatenate(st_parts, axis=1)
    ht_ref[...] = ht_new

    @pl.when(c == n_chunks - 1)
    def _():
        hnew_ref[...] = jnp.transpose(ht_new)

    y = jnp.concatenate(y_parts, axis=1) + dskip_ref[...] * xs
    y = y[0:valid] * _silu(z_ref[...])
    outs = []
    for grp in range(SSM_GROUPS):
        yg = y[:, grp * GROUP_WIDTH:(grp + 1) * GROUP_WIDTH]
        outs.append(yg * lax.rsqrt(jnp.mean(yg * yg, axis=-1, keepdims=True) + EPS))
    y_ref[...] = (jnp.concatenate(outs, axis=1) * gn_ref[...]).astype(y_ref.dtype)


def _ssd(xbc, dt_raw, z, conv_prev, h0, w):
    n_seq, rows, _ = xbc.shape
    valid = min(rows, CHUNK)
    n_chunks = rows // valid
    assert n_chunks * valid == rows and valid >= CONV_W - 1
    blk = lambda n: pl.BlockSpec((None, valid, n), lambda s, c: (s, c, 0))
    per_seq = lambda a: pl.BlockSpec((None,) + a.shape[1:], lambda s, c: (s, 0, 0))
    small = lambda a: pl.BlockSpec(a.shape, lambda s, c: (0, 0))
    consts = [w["conv_w"], w["conv_b"], w["dt_bias"], w["a_log"], w["d_skip"], w["gnorm"]]
    return pl.pallas_call(
        functools.partial(_ssd_body, valid=valid, n_chunks=n_chunks),
        grid=(n_seq, n_chunks),
        in_specs=[blk(CONV_DIM), blk(LANES), blk(SSM_WIDTH), per_seq(conv_prev), per_seq(h0)]
                 + [small(a) for a in consts],
        out_specs=[blk(SSM_WIDTH), per_seq(conv_prev), per_seq(h0)],
        out_shape=(jax.ShapeDtypeStruct((n_seq, rows, SSM_WIDTH), BF16),
                   jax.ShapeDtypeStruct(conv_prev.shape, F32),
                   jax.ShapeDtypeStruct(h0.shape, F32)),
        scratch_shapes=[pltpu.VMEM((CHUNK + TAIL, CONV_DIM), F32),
                        pltpu.VMEM((CHUNK, LANES), F32),
                        pltpu.VMEM((D_STATE, SSM_WIDTH), F32)],
        compiler_params=_cparams(2),
        name="ssd",
    )(xbc, dt_raw, z, conv_prev, h0, *consts)


def _outproj_body(att_ref, ssm_ref, x_ref, p_ref, woa_ref, wos_ref, wpp_ref, wpg_ref, fn_ref, y_ref):
    hmid = (x_ref[...]
            + jnp.dot(att_ref[...], woa_ref[...], preferred_element_type=F32)
            + jnp.dot(ssm_ref[...], wos_ref[...], preferred_element_type=F32))
    emb = jnp.dot(p_ref[...].astype(BF16), wpp_ref[...], preferred_element_type=F32)
    gate = jax.nn.sigmoid(jnp.dot(hmid.astype(BF16), wpg_ref[...], preferred_element_type=F32))
    out = hmid + emb * gate
    y_ref[...] = out * lax.rsqrt(jnp.mean(out * out, axis=-1, keepdims=True) + EPS) * fn_ref[...]


def _outproj(att, ssm, x, p, w, tm):
    rows = x.shape[0]
    row = lambda n: pl.BlockSpec((tm, n), lambda i: (i, 0))
    wspec = lambda a: pl.BlockSpec(a.shape, lambda i: (0, 0))
    ws = [w["out_att"], w["out_ssm"], w["ple_proj"], w["ple_gate"], w["final_norm"]]
    return pl.pallas_call(
        _outproj_body,
        grid=(rows // tm,),
        in_specs=[row(ATT_WIDTH), row(SSM_WIDTH), row(D_MODEL), row(PLE_DIM)] + [wspec(a) for a in ws],
        out_specs=row(D_MODEL),
        out_shape=jax.ShapeDtypeStruct((rows, D_MODEL), F32),
        compiler_params=_cparams(1),
        name="outproj",
    )(att, ssm, x, p, *ws)


def _rope_tables(pos):
    half = QK_DIM // 2
    inv = ROPE_THETA ** (-jnp.arange(0, QK_DIM, 2, dtype=F32) / QK_DIM)
    ang = pos.astype(F32)[:, None] * inv[None, :]
    cos = jnp.tile(jnp.cos(ang), (1, LANES // half))
    sin = jnp.sin(ang)
    sin = jnp.tile(jnp.concatenate([-sin, sin], axis=1), (1, LANES // QK_DIM))
    return cos, sin


def _pad_rows(a, rows):
    return jnp.pad(a, ((0, 0), (0, rows - a.shape[1]), (0, 0)))


def kernel(x_prompt, x_sample, cache_k, cache_v, state_conv, state_ssm, page_table, p_prompt, p_sample, w_norm, w_in, lambda_q1, lambda_k1, lambda_q2, lambda_k2, subln_w, conv_w, conv_b, dt_bias, A_log, D_skip, gnorm_w, w_out, w_ple_proj, w_ple_gate, final_norm_w):
    assert w_norm.shape[0] == 1 and x_prompt.shape[0] == 1
    seq = x_prompt.shape[1]
    dec_batch, dec_seq, _ = x_sample.shape
    n_pool, page = cache_k.shape[1], cache_k.shape[2]
    past = page_table.shape[1] * page

    w_in0 = w_in[0]
    splits = [0, 1024, 2048, 3072, 4096, 5120, 5120 + CONV_DIM, 5120 + CONV_DIM + SSM_HEADS]
    names = ["q", "k", "v", "g", "z", "xbc", "dt"]
    w = {n: w_in0[:, a:b].astype(BF16) for n, a, b in zip(names, splits[:-1], splits[1:])}
    w["dt"] = jnp.pad(w["dt"], ((0, 0), (0, LANES - SSM_HEADS)))
    w["norm"] = w_norm
    lane_pad = lambda v: jnp.pad(v, ((0, 0), (0, LANES - v.shape[1])))
    ssd_w = {"conv_w": conv_w[0], "conv_b": conv_b, "dt_bias": lane_pad(dt_bias),
             "a_log": lane_pad(A_log), "d_skip": jnp.repeat(D_skip, SSM_HEAD_DIM, axis=1),
             "gnorm": gnorm_w}
    out_w = {"out_att": w_out[0, :ATT_WIDTH].astype(BF16), "out_ssm": w_out[0, ATT_WIDTH:].astype(BF16),
             "ple_proj": w_ple_proj[0].astype(BF16), "ple_gate": w_ple_gate[0].astype(BF16),
             "final_norm": final_norm_w[None, :]}
    lams = [lambda_q1, lambda_k1, lambda_q2, lambda_k2]

    cos_p, sin_p = _rope_tables(jnp.arange(seq))
    q_p, k_p, kb_p, v_p, vb_p, g_p, z_p, xbc_p, dt_p = _inproj(x_prompt[0], w, cos_p, sin_p, tm=256)
    att_p = _prompt_attn(q_p, kb_p, vb_p, g_p, subln_w, lams, tq=256, tk=512)
    ssm_p, conv_p, h_p = _ssd(xbc_p[None], dt_p[None], z_p[None],
                              jnp.zeros((1, CONV_W - 1, CONV_DIM), F32),
                              jnp.zeros((1, SSM_WIDTH, D_STATE), F32), ssd_w)
    y_p = _outproj(att_p, ssm_p[0], x_prompt[0], p_prompt[0, 0], out_w, tm=512)

    n_rows = dec_batch * dec_seq
    cos_s, sin_s = _rope_tables(past + jnp.arange(n_rows) % dec_seq)
    q_s, k_s, _, v_s, _, g_s, z_s, xbc_s, dt_s = _inproj(x_sample.reshape(n_rows, D_MODEL), w,
                                                       cos_s, sin_s, tm=n_rows)
    per_seq = lambda a: a.reshape(dec_batch, dec_seq, a.shape[-1])
    q3 = per_seq(q_s)
    q16 = _pad_rows(jnp.concatenate([q3, q3], axis=1), QROWS)
    cache_kt = jnp.transpose(cache_k[0], (0, 2, 3, 4, 1)).reshape(n_pool, ATT_WIDTH, page)
    att_s = _decode_attn(page_table, q16, cache_kt, cache_v[0],
                         _pad_rows(per_seq(k_s), SUBLANES), _pad_rows(per_seq(v_s), SUBLANES),
                         _pad_rows(per_seq(g_s), SUBLANES), subln_w, lams, dec_seq)
    att_s = att_s[:, :dec_seq].reshape(n_rows, ATT_WIDTH).astype(BF16)
    ssm_s, conv_s, h_s = _ssd(per_seq(xbc_s), per_seq(dt_s), per_seq(z_s), state_conv[0],
                              state_ssm[0].reshape(dec_batch, SSM_WIDTH, D_STATE), ssd_w)
    y_s = _outproj(att_s, ssm_s.reshape(n_rows, SSM_WIDTH), x_sample.reshape(n_rows, D_MODEL),
                   p_sample[0].reshape(n_rows, PLE_DIM), out_w, tm=n_rows)

    hp = (SSM_HEADS, SSM_HEAD_DIM, D_STATE)
    return (y_p[None],
            y_s.reshape(dec_batch, dec_seq, D_MODEL),
            k_p.reshape(1, 1, seq, ATT_HEADS, 2, QK_DIM),
            v_p.reshape(1, 1, seq, ATT_HEADS, V_DIM),
            conv_p[None],
            h_p.reshape((1, 1) + hp),
            k_s.reshape(1, dec_batch, dec_seq, ATT_HEADS, 2, QK_DIM),
            v_s.reshape(1, dec_batch, dec_seq, ATT_HEADS, V_DIM),
            conv_s[None],
            h_s.reshape((1, dec_batch) + hp))
```

```python
import functools
import math

import jax
import jax.numpy as jnp
from jax import lax
from jax.experimental import pallas as pl
from jax.experimental.pallas import tpu as pltpu

F32 = jnp.float32
BF16 = jnp.bfloat16

D_MODEL = 1024
ATT_HEADS = 8
QK_DIM = 64
V_DIM = 2 * QK_DIM
ATT_WIDTH = ATT_HEADS * V_DIM
SM_SCALE = QK_DIM ** -0.5
LOG2E = math.log2(math.e)
ROPE_THETA = 10000.0
SSM_WIDTH = 1024
SSM_HEAD_DIM = 64
SSM_HEADS = SSM_WIDTH // SSM_HEAD_DIM
SSM_GROUPS = 2
D_STATE = 128
CONV_W = 4
CONV_DIM = SSM_WIDTH + 2 * SSM_GROUPS * D_STATE
CHUNK = 128
PLE_DIM = 256
EPS = 1e-6
LAM_INIT = 0.8 - 0.6 * math.exp(-0.3 * 0)

LANES = 128
SUBLANES = 8
BF16_ROWS = 16
NEG_BIG = -1e30
VMEM_LIMIT = 56 * 1024 * 1024

INPROJ_TM = 256
OUTPROJ_TM = 512
ATTN_TQ = 512
ATTN_TK = 512
DECODE_PAGES = 8


def _cparams(n_axes):
    return pltpu.CompilerParams(dimension_semantics=("arbitrary",) * n_axes,
                                vmem_limit_bytes=VMEM_LIMIT)


def _silu(x):
    return x * jax.nn.sigmoid(x)


def _inproj_body(x_ref, wn_ref, wq_ref, wk_ref, wv_ref, wg_ref, wz_ref, wx_ref, wdt_ref,
                 cos_ref, sin_ref, q_ref, k_ref, v_ref, g_ref, z_ref, xbc_ref, dt_ref,
                 *attn_copies):
    x = x_ref[...]
    hn = x * lax.rsqrt(jnp.mean(x * x, axis=-1, keepdims=True) + EPS) * wn_ref[...]
    hn = hn.astype(BF16)
    cos = cos_ref[...]
    sin = sin_ref[...]
    lane = lax.broadcasted_iota(jnp.int32, cos.shape, 1)
    first_half = (lane % QK_DIM) < (QK_DIM // 2)

    def rope_chunks(t):
        for c in range(t.shape[1] // LANES):
            tc = t[:, c * LANES:(c + 1) * LANES]
            partner = jnp.where(first_half,
                                pltpu.roll(tc, LANES - QK_DIM // 2, 1),
                                pltpu.roll(tc, QK_DIM // 2, 1))
            yield c, tc * cos + partner * sin

    q = jnp.dot(hn, wq_ref[...], preferred_element_type=F32)
    for c, r in rope_chunks(q):
        q_ref[:, c * LANES:(c + 1) * LANES] = (r * (SM_SCALE * LOG2E)).astype(BF16)
    k = jnp.dot(hn, wk_ref[...], preferred_element_type=F32)
    for c, r in rope_chunks(k):
        k_ref[:, c * LANES:(c + 1) * LANES] = r
        if attn_copies:
            attn_copies[0][:, c * LANES:(c + 1) * LANES] = r.astype(BF16)
    v = jnp.dot(hn, wv_ref[...], preferred_element_type=F32)
    v_ref[...] = v
    if attn_copies:
        attn_copies[1][...] = jnp.transpose(v).astype(BF16)
    g_ref[...] = jnp.dot(hn, wg_ref[...], preferred_element_type=F32)
    z_ref[...] = jnp.dot(hn, wz_ref[...], preferred_element_type=F32)
    xbc_ref[...] = jnp.dot(hn, wx_ref[...], preferred_element_type=F32)
    dt_ref[...] = jnp.dot(hn, wdt_ref[...], preferred_element_type=F32)


def _inproj(x, w, cos, sin, tm, attn_tk=None):
    rows = x.shape[0]
    row = lambda n: pl.BlockSpec((tm, n), lambda i: (i, 0))
    wspec = lambda a: pl.BlockSpec(a.shape, lambda i: (0, 0), pipeline_mode=pl.Buffered(1))
    out_shapes = [
        jax.ShapeDtypeStruct((rows, ATT_WIDTH), BF16),
        jax.ShapeDtypeStruct((rows, ATT_WIDTH), F32),
        jax.ShapeDtypeStruct((rows, ATT_WIDTH), F32),
        jax.ShapeDtypeStruct((rows, ATT_WIDTH), F32),
        jax.ShapeDtypeStruct((rows, SSM_WIDTH), F32),
        jax.ShapeDtypeStruct((rows, CONV_DIM), F32),
        jax.ShapeDtypeStruct((rows, LANES), F32),
    ]
    out_specs = [row(s.shape[1]) for s in out_shapes]
    if attn_tk is not None:
        per_tile = attn_tk // tm
        assert per_tile * tm == attn_tk and rows % attn_tk == 0
        out_shapes += [jax.ShapeDtypeStruct((rows, ATT_WIDTH), BF16),
                       jax.ShapeDtypeStruct((rows // attn_tk, ATT_WIDTH, attn_tk), BF16)]
        out_specs += [row(ATT_WIDTH),
                      pl.BlockSpec((None, ATT_WIDTH, tm), lambda i: (i // per_tile, 0, i % per_tile))]
    return pl.pallas_call(
        _inproj_body,
        grid=(rows // tm,),
        in_specs=[row(D_MODEL), wspec(w["norm"]), wspec(w["q"]), wspec(w["k"]), wspec(w["v"]),
                  wspec(w["g"]), wspec(w["z"]), wspec(w["xbc"]), wspec(w["dt"]),
                  row(LANES), row(LANES)],
        out_specs=out_specs,
        out_shape=out_shapes,
        compiler_params=_cparams(1),
        name="inproj",
    )(x, w["norm"], w["q"], w["k"], w["v"], w["g"], w["z"], w["xbc"], w["dt"], cos, sin)


def _lambda_value(lq1_ref, lk1_ref, lq2_ref, lk2_ref):
    a = jnp.sum(lq1_ref[...] * lk1_ref[...], axis=-1, keepdims=True)
    b = jnp.sum(lq2_ref[...] * lk2_ref[...], axis=-1, keepdims=True)
    return jnp.exp(a) - jnp.exp(b) + LAM_INIT


def _sub_norm_gate(o, sub_w, g):
    o = o * lax.rsqrt(jnp.mean(o * o, axis=-1, keepdims=True) + EPS) * sub_w * (1.0 - LAM_INIT)
    return o * _silu(g)


def _prompt_attn_body(q_ref, k_ref, vt_ref, g_ref, sub_ref, lq1_ref, lk1_ref, lq2_ref, lk2_ref,
                      o_ref, m_ref, acc_ref, sa_ref, sb_ref, *, tq, tk):
    i = pl.program_id(1)
    qt = jnp.transpose(q_ref[...].astype(F32))
    dim = lax.broadcasted_iota(jnp.int32, qt.shape, 0)
    q_blk = jnp.concatenate([jnp.where(dim < QK_DIM, qt, 0.0),
                             jnp.where(dim >= QK_DIM, qt, 0.0)], axis=1).astype(BF16)
    m_ref[...] = jnp.full(m_ref.shape, NEG_BIG, F32)
    acc_ref[...] = jnp.zeros(acc_ref.shape, F32)
    ones = jnp.ones((BF16_ROWS, tk), BF16)

    def scores(t):
        kt = k_ref[pl.ds(pl.multiple_of(t * tk, tk), tk), :]
        return jnp.dot(kt, q_blk, preferred_element_type=F32)

    def update(s, t):
        m_prev = m_ref[...]
        m_new = jnp.maximum(m_prev, jnp.max(s, axis=0, keepdims=True))
        alpha = jnp.exp2(m_prev - m_new)
        p = jnp.exp2(s - m_new).astype(BF16)
        v_ext = jnp.concatenate([vt_ref[t], ones], axis=0)
        acc_ref[...] = alpha * acc_ref[...] + jnp.dot(v_ext, p, preferred_element_type=F32)
        m_ref[...] = m_new

    def masked(s, t):
        kpos = t * tk + lax.broadcasted_iota(jnp.int32, s.shape, 0)
        qpos = i * tq + lax.broadcasted_iota(jnp.int32, s.shape, 1) % tq
        return jnp.where(kpos <= qpos, s, NEG_BIG)

    n_full = (i * tq) // tk
    sa_ref[...] = scores(0)

    def pair(tt, carry):
        sb_ref[...] = scores(2 * tt + 1)
        update(sa_ref[...], 2 * tt)
        sa_ref[...] = scores(2 * tt + 2)
        update(sb_ref[...], 2 * tt + 1)
        return carry

    lax.fori_loop(0, n_full // 2, pair, 0)

    @pl.when(n_full % 2 == 1)
    def _():
        sb_ref[...] = scores(n_full)
        update(sa_ref[...], n_full - 1)
        update(masked(sb_ref[...], n_full), n_full)

    @pl.when(n_full % 2 == 0)
    def _():
        update(masked(sa_ref[...], n_full), n_full)

    acc = acc_ref[...]
    o1 = acc[:V_DIM, :tq] / acc[V_DIM:V_DIM + 1, :tq]
    o2 = acc[:V_DIM, tq:] / acc[V_DIM:V_DIM + 1, tq:]
    lam = _lambda_value(lq1_ref, lk1_ref, lq2_ref, lk2_ref)
    o = jnp.transpose(o1 - lam * o2)
    o_ref[...] = _sub_norm_gate(o, sub_ref[...], g_ref[...]).astype(o_ref.dtype)


def _prompt_attn(q, k, vt, g, sub_w, lams, tq, tk):
    seq = q.shape[0]
    assert seq % tq == 0 and seq % tk == 0 and vt.shape == (seq // tk, ATT_WIDTH, tk)
    head_tile = pl.BlockSpec((tq, V_DIM), lambda h, i: (i, h))
    k_all = pl.BlockSpec((seq, V_DIM), lambda h, i: (0, h))
    vt_all = pl.BlockSpec((seq // tk, V_DIM, tk), lambda h, i: (0, h, 0))
    small = lambda a: pl.BlockSpec(a.shape, lambda h, i: (0, 0))
    return pl.pallas_call(
        functools.partial(_prompt_attn_body, tq=tq, tk=tk),
        grid=(ATT_HEADS, seq // tq),
        in_specs=[head_tile, k_all, vt_all, head_tile, small(sub_w)] + [small(a) for a in lams],
        out_specs=head_tile,
        out_shape=jax.ShapeDtypeStruct((seq, ATT_WIDTH), BF16),
        scratch_shapes=[pltpu.VMEM((1, 2 * tq), F32),
                        pltpu.VMEM((V_DIM + BF16_ROWS, 2 * tq), F32),
                        pltpu.VMEM((tk, 2 * tq), F32),
                        pltpu.VMEM((tk, 2 * tq), F32)],
        compiler_params=_cparams(2),
        name="prompt_attn",
    )(q, k, vt, g, sub_w, *lams)


XROWS = ATT_HEADS * SUBLANES


def _decode_attn_body(pt_ref, *refs, n_pages, dec_seq, group):
    del pt_ref
    q_ref = refs[0]
    k_refs = refs[1:1 + group]
    v_refs = refs[1 + group:1 + 2 * group]
    (kn_ref, vn_ref, g_ref, sub_ref, lq1_ref, lk1_ref, lq2_ref, lk2_ref,
     o_ref, m_ref, l_ref, acc_ref, qblk_ref) = refs[1 + 2 * group:]
    step = pl.program_id(1)
    page = LANES

    @pl.when(step == 0)
    def _():
        m_ref[...] = jnp.full(m_ref.shape, NEG_BIG, F32)
        l_ref[...] = jnp.zeros(l_ref.shape, F32)
        acc_ref[...] = jnp.zeros(acc_ref.shape, F32)
        q8 = q_ref[...].astype(F32)
        row = lax.broadcasted_iota(jnp.int32, q8.shape, 0)
        lane = lax.broadcasted_iota(jnp.int32, q8.shape, 1)
        own_map = ((lane % V_DIM) < QK_DIM) == (row < dec_seq)
        for h in range(ATT_HEADS):
            keep = own_map & ((lane // V_DIM) == h)
            qblk_ref[h * SUBLANES:(h + 1) * SUBLANES, :] = jnp.where(keep, q8, 0.0).astype(BF16)

    qblk = qblk_ref[...]
    first_of_pair = lax.broadcasted_iota(jnp.int32, (BF16_ROWS, V_DIM), 0) < SUBLANES

    def process(k_pages, v_pages, new_tokens):
        s = jnp.concatenate([jnp.dot(qblk, kp[...].astype(BF16), preferred_element_type=F32)
                             for kp in k_pages], axis=1)
        if new_tokens:
            qi = lax.broadcasted_iota(jnp.int32, s.shape, 0) % dec_seq
            kj = lax.broadcasted_iota(jnp.int32, s.shape, 1)
            s = jnp.where(kj <= qi, s, NEG_BIG)
        m_prev = m_ref[...]
        m_new = jnp.maximum(m_prev, jnp.max(s, axis=-1, keepdims=True))
        alpha = jnp.exp2(m_prev - m_new)
        p = jnp.exp2(s - m_new)
        l_ref[...] = alpha * l_ref[...] + jnp.sum(p, axis=-1, keepdims=True)
        p = p.astype(BF16)
        pv_parts = []
        for pair in range(ATT_HEADS // 2):
            p_pair = p[pair * BF16_ROWS:(pair + 1) * BF16_ROWS]
            halves = []
            for h in (2 * pair, 2 * pair + 1):
                vh = jnp.concatenate([vp[pl.ds(h, page, stride=ATT_HEADS), :] for vp in v_pages], axis=0)
                halves.append(jnp.dot(p_pair, vh.astype(BF16), preferred_element_type=F32))
            pv_parts.append(jnp.where(first_of_pair, halves[0], halves[1]))
        acc_ref[...] = alpha * acc_ref[...] + jnp.concatenate(pv_parts, axis=0)
        m_ref[...] = m_new

    process(k_refs, v_refs, False)

    @pl.when(step == n_pages // group - 1)
    def _():
        process([kn_ref], [vn_ref], True)
        t = acc_ref[...] / l_ref[...]
        lam = _lambda_value(lq1_ref, lk1_ref, lq2_ref, lk2_ref)
        g = g_ref[...]
        for h in range(ATT_HEADS):
            cols = slice(h * V_DIM, (h + 1) * V_DIM)
            th = t[h * SUBLANES:(h + 1) * SUBLANES]
            o = th - lam * pltpu.roll(th, SUBLANES - dec_seq, 0)
            o_ref[:, cols] = _sub_norm_gate(o, sub_ref[...], g[:, cols])


def _decode_attn(page_table, q8, cache_kt, cache_vr, kt_new, vr_new, g8, sub_w, lams, dec_seq, group):
    batch, n_pages = page_table.shape
    page = cache_kt.shape[2]
    assert page == LANES and 2 * dec_seq == SUBLANES and n_pages % group == 0
    per_b = lambda a: pl.BlockSpec((None,) + a.shape[1:], lambda b, p, pt: (b, 0, 0))
    paged = lambda a, n: pl.BlockSpec((None,) + a.shape[1:],
                                      lambda b, p, pt: (pt[b, p * group + n], 0, 0))
    small = lambda a: pl.BlockSpec(a.shape, lambda b, p, pt: (0, 0))
    grid_spec = pltpu.PrefetchScalarGridSpec(
        num_scalar_prefetch=1,
        grid=(batch, n_pages // group),
        in_specs=[per_b(q8)] + [paged(cache_kt, n) for n in range(group)]
                 + [paged(cache_vr, n) for n in range(group)]
                 + [per_b(kt_new), per_b(vr_new), per_b(g8), small(sub_w)]
                 + [small(a) for a in lams],
        out_specs=per_b(g8),
        scratch_shapes=[pltpu.VMEM((XROWS, 1), F32), pltpu.VMEM((XROWS, 1), F32),
                        pltpu.VMEM((XROWS, V_DIM), F32), pltpu.VMEM((XROWS, ATT_WIDTH), BF16)],
    )
    return pl.pallas_call(
        functools.partial(_decode_attn_body, n_pages=n_pages, dec_seq=dec_seq, group=group),
        grid_spec=grid_spec,
        out_shape=jax.ShapeDtypeStruct(g8.shape, F32),
        compiler_params=_cparams(2),
        name="decode_attn",
    )(page_table, q8, *([cache_kt] * group), *([cache_vr] * group), kt_new, vr_new, g8, sub_w, *lams)


HEADS_PER_GROUP = SSM_HEADS // SSM_GROUPS
GROUP_WIDTH = SSM_WIDTH // SSM_GROUPS
TAIL = SUBLANES


def _pair_expand(a, j):
    lane = lax.broadcasted_iota(jnp.int32, (a.shape[0], LANES), 1)
    return jnp.where(lane < SSM_HEAD_DIM, a[:, 2 * j:2 * j + 1], a[:, 2 * j + 1:2 * j + 2])


def _ssd_body(xbc_ref, dt_ref, z_ref, cprev_ref, h0_ref, cw_ref, cb_ref, dtb_ref, alog_ref,
              dskip_ref, gn_ref,
              y_ref, cnew_ref, hnew_ref,
              xs_ref, dtp_ref, ht_ref, *, valid, n_chunks):
    q = CHUNK
    c = pl.program_id(1)
    first_step = (pl.program_id(0) == 0) & (c == 0)

    @pl.when(first_step)
    def _():
        xs_ref[...] = jnp.zeros(xs_ref.shape, F32)
        dtp_ref[...] = jnp.zeros(dtp_ref.shape, F32)

    @pl.when(c == 0)
    def _():
        xs_ref[TAIL - (CONV_W - 1):TAIL, :] = cprev_ref[...]
        ht_ref[...] = jnp.transpose(h0_ref[...])

    xs_ref[TAIL:TAIL + valid, :] = xbc_ref[...]
    dtp_ref[0:valid, :] = dt_ref[...]

    u = cb_ref[...]
    for w in range(CONV_W):
        u = u + xs_ref[pl.ds(TAIL - (CONV_W - 1) + w, q), :] * cw_ref[w:w + 1, :]
    u = _silu(u)

    @pl.when(c == n_chunks - 1)
    def _():
        cnew_ref[...] = xs_ref[pl.ds(TAIL + valid - (CONV_W - 1), CONV_W - 1), :]

    if n_chunks > 1:
        xs_ref[0:TAIL, :] = xs_ref[q:q + TAIL, :]

    xs = u[:, :SSM_WIDTH]
    bm = u[:, SSM_WIDTH:SSM_WIDTH + SSM_GROUPS * D_STATE]
    cm = u[:, SSM_WIDTH + SSM_GROUPS * D_STATE:]

    row = lax.broadcasted_iota(jnp.int32, (q, LANES), 0)
    x_dt = dtp_ref[...] + dtb_ref[...]
    dt = jnp.maximum(x_dt, 0.0) + jnp.log1p(jnp.exp(-jnp.abs(x_dt)))
    if valid < q:
        dt = jnp.where(row < valid, dt, 0.0)
    a = -jnp.exp(alog_ref[...])
    da = dt * a

    tt = lax.broadcasted_iota(jnp.int32, (q, q), 0)
    ss = lax.broadcasted_iota(jnp.int32, (q, q), 1)
    causal = ss <= tt
    tril = jnp.where(causal, 1.0, 0.0).astype(BF16)
    da_hi = da.astype(BF16)
    r1 = da - da_hi.astype(F32)
    da_mid = r1.astype(BF16)
    da_lo = (r1 - da_mid.astype(F32)).astype(BF16)
    cs3 = jnp.dot(tril, jnp.concatenate([da_hi, da_mid, da_lo], axis=1), preferred_element_type=F32)
    cs = cs3[:, :LANES] + cs3[:, LANES:2 * LANES] + cs3[:, 2 * LANES:]

    cs_t = jnp.transpose(cs)
    dt_t = jnp.transpose(dt)
    cs_last = cs[q - 1:q, :]
    e_cs = jnp.exp(cs)
    dec_end = jnp.exp(cs_last - cs) * dt
    chunk_decay = jnp.exp(cs_last)

    ht = ht_ref[...]
    ht_b = ht.astype(BF16)
    y_parts = []
    st_parts = []
    for grp in range(SSM_GROUPS):
        bg = bm[:, grp * D_STATE:(grp + 1) * D_STATE]
        cg = cm[:, grp * D_STATE:(grp + 1) * D_STATE].astype(BF16)
        cb = lax.dot_general(cg, bg.astype(BF16), (((1,), (1,)), ((), ())),
                             preferred_element_type=F32)
        y_off = jnp.dot(cg, ht_b[:, grp * GROUP_WIDTH:(grp + 1) * GROUP_WIDTH],
                        preferred_element_type=F32)
        xd_parts = []
        for jp in range(HEADS_PER_GROUP // 2):
            j = grp * (HEADS_PER_GROUP // 2) + jp
            x_pair = xs[:, j * LANES:(j + 1) * LANES]
            x_pair_b = x_pair.astype(BF16)
            y_pair = []
            for h in (2 * j, 2 * j + 1):
                diff = cs[:, h:h + 1] - cs_t[h:h + 1, :]
                lmat = jnp.exp(jnp.where(causal, diff, NEG_BIG))
                wmat = (cb * lmat * dt_t[h:h + 1, :]).astype(BF16)
                y_pair.append(jnp.dot(wmat, x_pair_b, preferred_element_type=F32))
            lane = lax.broadcasted_iota(jnp.int32, (q, LANES), 1)
            y_diag = jnp.where(lane < SSM_HEAD_DIM, y_pair[0], y_pair[1])
            y_parts.append(y_diag + y_off[:, jp * LANES:(jp + 1) * LANES] * _pair_expand(e_cs, j))
            xd_parts.append((x_pair * _pair_expand(dec_end, j)).astype(BF16))
        xd = jnp.concatenate(xd_parts, axis=1)
        st_parts.append(jnp.dot(jnp.transpose(bg).astype(BF16), xd, preferred_element_type=F32))
    decay_lanes = jnp.concatenate([_pair_expand(chunk_decay, j) for j in range(SSM_HEADS // 2)], axis=1)
    ht_new = ht * decay_lanes + jnp.concatenate(st_parts, axis=1)
    ht_ref[...] = ht_new

    @pl.when(c == n_chunks - 1)
    def _():
        hnew_ref[...] = jnp.transpose(ht_new)

    y = jnp.concatenate(y_parts, axis=1) + dskip_ref[...] * xs
    y = y[0:valid] * _silu(z_ref[...])
    outs = []
    for grp in range(SSM_GROUPS):
        yg = y[:, grp * GROUP_WIDTH:(grp + 1) * GROUP_WIDTH]
        outs.append(yg * lax.rsqrt(jnp.mean(yg * yg, axis=-1, keepdims=True) + EPS))
    y_ref[...] = (jnp.concatenate(outs, axis=1) * gn_ref[...]).astype(y_ref.dtype)


def _ssd(xbc, dt_raw, z, conv_prev, h0, w):
    n_seq, rows, _ = xbc.shape
    valid = min(rows, CHUNK)
    n_chunks = rows // valid
    assert n_chunks * valid == rows and valid >= CONV_W - 1
    blk = lambda n: pl.BlockSpec((None, valid, n), lambda s, c: (s, c, 0))
    per_seq = lambda a: pl.BlockSpec((None,) + a.shape[1:], lambda s, c: (s, 0, 0))
    small = lambda a: pl.BlockSpec(a.shape, lambda s, c: (0, 0))
    consts = [w["conv_w"], w["conv_b"], w["dt_bias"], w["a_log"], w["d_skip"], w["gnorm"]]
    return pl.pallas_call(
        functools.partial(_ssd_body, valid=valid, n_chunks=n_chunks),
        grid=(n_seq, n_chunks),
        in_specs=[blk(CONV_DIM), blk(LANES), blk(SSM_WIDTH), per_seq(conv_prev), per_seq(h0)]
                 + [small(a) for a in consts],
        out_specs=[blk(SSM_WIDTH), per_seq(conv_prev), per_seq(h0)],
        out_shape=(jax.ShapeDtypeStruct((n_seq, rows, SSM_WIDTH), BF16),
                   jax.ShapeDtypeStruct(conv_prev.shape, F32),
                   jax.ShapeDtypeStruct(h0.shape, F32)),
        scratch_shapes=[pltpu.VMEM((CHUNK + TAIL, CONV_DIM), F32),
                        pltpu.VMEM((CHUNK, LANES), F32),
                        pltpu.VMEM((D_STATE, SSM_WIDTH), F32)],
        compiler_params=_cparams(2),
        name="ssd",
    )(xbc, dt_raw, z, conv_prev, h0, *consts)


def _outproj_body(att_ref, ssm_ref, x_ref, p_ref, woa_ref, wos_ref, wpp_ref, wpg_ref, fn_ref, y_ref):
    hmid = (x_ref[...]
            + jnp.dot(att_ref[...], woa_ref[...], preferred_element_type=F32)
            + jnp.dot(ssm_ref[...], wos_ref[...], preferred_element_type=F32))
    emb = jnp.dot(p_ref[...].astype(BF16), wpp_ref[...], preferred_element_type=F32)
    gate = jax.nn.sigmoid(jnp.dot(hmid.astype(BF16), wpg_ref[...], preferred_element_type=F32))
    out = hmid + emb * gate
    y_ref[...] = out * lax.rsqrt(jnp.mean(out * out, axis=-1, keepdims=True) + EPS) * fn_ref[...]


def _outproj(att, ssm, x, p, w, tm):
    rows = x.shape[0]
    row = lambda n: pl.BlockSpec((tm, n), lambda i: (i, 0))
    wspec = lambda a: pl.BlockSpec(a.shape, lambda i: (0, 0))
    ws = [w["out_att"], w["out_ssm"], w["ple_proj"], w["ple_gate"], w["final_norm"]]
    return pl.pallas_call(
        _outproj_body,
        grid=(rows // tm,),
        in_specs=[row(ATT_WIDTH), row(SSM_WIDTH), row(D_MODEL), row(PLE_DIM)] + [wspec(a) for a in ws],
        out_specs=row(D_MODEL),
        out_shape=jax.ShapeDtypeStruct((rows, D_MODEL), F32),
        compiler_params=_cparams(1),
        name="outproj",
    )(att, ssm, x, p, *ws)


def _rope_tables(pos):
    half = QK_DIM // 2
    inv = ROPE_THETA ** (-jnp.arange(0, QK_DIM, 2, dtype=F32) / QK_DIM)
    ang = pos.astype(F32)[:, None] * inv[None, :]
    cos = jnp.tile(jnp.cos(ang), (1, LANES // half))
    sin = jnp.sin(ang)
    sin = jnp.tile(jnp.concatenate([-sin, sin], axis=1), (1, LANES // QK_DIM))
    return cos, sin


def _pad_to(a, axis, size):
    pads = [(0, 0)] * a.ndim
    pads[axis] = (0, size - a.shape[axis])
    return jnp.pad(a, pads)


def kernel(x_prompt, x_sample, cache_k, cache_v, state_conv, state_ssm, page_table, p_prompt, p_sample, w_norm, w_in, lambda_q1, lambda_k1, lambda_q2, lambda_k2, subln_w, conv_w, conv_b, dt_bias, A_log, D_skip, gnorm_w, w_out, w_ple_proj, w_ple_gate, final_norm_w):
    assert w_norm.shape[0] == 1 and x_prompt.shape[0] == 1
    seq = x_prompt.shape[1]
    dec_batch, dec_seq, _ = x_sample.shape
    n_pool, page = cache_k.shape[1], cache_k.shape[2]
    past = page_table.shape[1] * page

    w_in0 = w_in[0]
    splits = [0, 1024, 2048, 3072, 4096, 5120, 5120 + CONV_DIM, 5120 + CONV_DIM + SSM_HEADS]
    names = ["q", "k", "v", "g", "z", "xbc", "dt"]
    w = {n: w_in0[:, a:b].astype(BF16) for n, a, b in zip(names, splits[:-1], splits[1:])}
    w["dt"] = _pad_to(w["dt"], 1, LANES)
    w["norm"] = w_norm
    ssd_w = {"conv_w": conv_w[0], "conv_b": conv_b, "dt_bias": _pad_to(dt_bias, 1, LANES),
             "a_log": _pad_to(A_log, 1, LANES), "d_skip": jnp.repeat(D_skip, SSM_HEAD_DIM, axis=1),
             "gnorm": gnorm_w}
    out_w = {"out_att": w_out[0, :ATT_WIDTH].astype(BF16), "out_ssm": w_out[0, ATT_WIDTH:].astype(BF16),
             "ple_proj": w_ple_proj[0].astype(BF16), "ple_gate": w_ple_gate[0].astype(BF16),
             "final_norm": final_norm_w[None, :]}
    lams = [lambda_q1, lambda_k1, lambda_q2, lambda_k2]

    cos_p, sin_p = _rope_tables(jnp.arange(seq))
    q_p, k_p, v_p, g_p, z_p, xbc_p, dt_p, kb_p, vt_p = _inproj(
        x_prompt[0], w, cos_p, sin_p, tm=INPROJ_TM, attn_tk=ATTN_TK)
    att_p = _prompt_attn(q_p, kb_p, vt_p, g_p, subln_w, lams, tq=ATTN_TQ, tk=ATTN_TK)
    ssm_p, conv_p, h_p = _ssd(xbc_p[None], dt_p[None], z_p[None],
                              jnp.zeros((1, CONV_W - 1, CONV_DIM), F32),
                              jnp.zeros((1, SSM_WIDTH, D_STATE), F32), ssd_w)
    y_p = _outproj(att_p, ssm_p[0], x_prompt[0], p_prompt[0, 0], out_w, tm=OUTPROJ_TM)

    n_rows = dec_batch * dec_seq
    cos_s, sin_s = _rope_tables(past + jnp.arange(n_rows) % dec_seq)
    q_s, k_s, v_s, g_s, z_s, xbc_s, dt_s = _inproj(x_sample.reshape(n_rows, D_MODEL), w,
                                                 cos_s, sin_s, tm=n_rows)
    per_seq = lambda a: a.reshape(dec_batch, dec_seq, a.shape[-1])
    q3 = per_seq(q_s)
    cache_kt = jnp.transpose(cache_k[0], (0, 2, 3, 4, 1)).reshape(n_pool, ATT_WIDTH, page)
    cache_vr = cache_v[0].reshape(n_pool, page * ATT_HEADS, V_DIM)
    kt_new = _pad_to(jnp.transpose(per_seq(k_s), (0, 2, 1)), 2, page)
    vr_new = _pad_to(v_s.reshape(dec_batch, dec_seq * ATT_HEADS, V_DIM), 1, page * ATT_HEADS)
    att_s = _decode_attn(page_table, jnp.concatenate([q3, q3], axis=1).astype(F32), cache_kt, cache_vr,
                         kt_new, vr_new, _pad_to(per_seq(g_s), 1, SUBLANES), subln_w, lams,
                         dec_seq, DECODE_PAGES)
    att_s = att_s[:, :dec_seq].reshape(n_rows, ATT_WIDTH).astype(BF16)
    ssm_s, conv_s, h_s = _ssd(per_seq(xbc_s), per_seq(dt_s), per_seq(z_s), state_conv[0],
                              state_ssm[0].reshape(dec_batch, SSM_WIDTH, D_STATE), ssd_w)
    y_s = _outproj(att_s, ssm_s.reshape(n_rows, SSM_WIDTH), x_sample.reshape(n_rows, D_MODEL),
                   p_sample[0].reshape(n_rows, PLE_DIM), out_w, tm=n_rows)

    hp = (SSM_HEADS, SSM_HEAD_DIM, D_STATE)
    return (y_p[None],
            y_s.reshape(dec_batch, dec_seq, D_MODEL),
            k_p.reshape(1, 1, seq, ATT_HEADS, 2, QK_DIM),
            v_p.reshape(1, 1, seq, ATT_HEADS, V_DIM),
            conv_p[None],
            h_p.reshape((1, 1) + hp),
            k_s.reshape(1, dec_batch, dec_seq, ATT_HEADS, 2, QK_DIM),
            v_s.reshape(1, dec_batch, dec_seq, ATT_HEADS, V_DIM),
            conv_s[None],
            h_s.reshape((1, dec_batch) + hp))
```

```python
import functools
import math

import jax
import jax.numpy as jnp
from jax import lax
from jax.experimental import pallas as pl
from jax.experimental.pallas import tpu as pltpu

F32 = jnp.float32
BF16 = jnp.bfloat16

D_MODEL = 1024
ATT_HEADS = 8
QK_DIM = 64
V_DIM = 2 * QK_DIM
ATT_WIDTH = ATT_HEADS * V_DIM
SM_SCALE = QK_DIM ** -0.5
LOG2E = math.log2(math.e)
ROPE_THETA = 10000.0
SSM_WIDTH = 1024
SSM_HEAD_DIM = 64
SSM_HEADS = SSM_WIDTH // SSM_HEAD_DIM
SSM_GROUPS = 2
D_STATE = 128
CONV_W = 4
CONV_DIM = SSM_WIDTH + 2 * SSM_GROUPS * D_STATE
CHUNK = 128
PLE_DIM = 256
EPS = 1e-6
LAM_INIT = 0.8 - 0.6 * math.exp(-0.3 * 0)

LANES = 128
SUBLANES = 8
BF16_ROWS = 16
NEG_BIG = -1e30
VMEM_LIMIT = 56 * 1024 * 1024

INPROJ_TM = 256
OUTPROJ_TM = 512
ATTN_TQ = 512
ATTN_TK = 512
UNROLL = 4
DECODE_PAGES = 16


def _cparams(n_axes):
    return pltpu.CompilerParams(dimension_semantics=("arbitrary",) * n_axes,
                                vmem_limit_bytes=VMEM_LIMIT)


def _silu(x):
    return x * jax.nn.sigmoid(x)


def _inproj_body(x_ref, wn_ref, wq_ref, wk_ref, wv_ref, wg_ref, wz_ref, wx_ref, wdt_ref,
                 cos_ref, sin_ref, q_ref, k_ref, v_ref, g_ref, z_ref, xbc_ref, dt_ref,
                 *attn_copies):
    x = x_ref[...]
    hn = x * lax.rsqrt(jnp.mean(x * x, axis=-1, keepdims=True) + EPS) * wn_ref[...]
    hn = hn.astype(BF16)
    cos = cos_ref[...]
    sin = sin_ref[...]
    lane = lax.broadcasted_iota(jnp.int32, cos.shape, 1)
    first_half = (lane % QK_DIM) < (QK_DIM // 2)

    def rope_chunks(t):
        for c in range(t.shape[1] // LANES):
            tc = t[:, c * LANES:(c + 1) * LANES]
            partner = jnp.where(first_half,
                                pltpu.roll(tc, LANES - QK_DIM // 2, 1),
                                pltpu.roll(tc, QK_DIM // 2, 1))
            yield c, tc * cos + partner * sin

    q = jnp.dot(hn, wq_ref[...], preferred_element_type=F32)
    for c, r in rope_chunks(q):
        r = r * (SM_SCALE * LOG2E)
        if attn_copies:
            q_ref[c] = r.astype(BF16)
        else:
            q_ref[:, c * LANES:(c + 1) * LANES] = r.astype(BF16)
    k = jnp.dot(hn, wk_ref[...], preferred_element_type=F32)
    for c, r in rope_chunks(k):
        k_ref[:, c * LANES:(c + 1) * LANES] = r
        if attn_copies:
            attn_copies[0][c] = r.astype(BF16)
    v = jnp.dot(hn, wv_ref[...], preferred_element_type=F32)
    v_ref[...] = v
    if attn_copies:
        attn_copies[1][...] = jnp.transpose(v).astype(BF16)
    g_ref[...] = jnp.dot(hn, wg_ref[...], preferred_element_type=F32)
    z_ref[...] = jnp.dot(hn, wz_ref[...], preferred_element_type=F32)
    xbc_ref[...] = jnp.dot(hn, wx_ref[...], preferred_element_type=F32)
    dt_ref[...] = jnp.dot(hn, wdt_ref[...], preferred_element_type=F32)


def _inproj(x, w, cos, sin, tm, attn_tk=None):
    rows = x.shape[0]
    row = lambda n: pl.BlockSpec((tm, n), lambda i: (i, 0))
    wspec = lambda a: pl.BlockSpec(a.shape, lambda i: (0, 0), pipeline_mode=pl.Buffered(1))
    q_shape = (ATT_HEADS, rows, V_DIM) if attn_tk is not None else (rows, ATT_WIDTH)
    out_shapes = [
        jax.ShapeDtypeStruct(q_shape, BF16),
        jax.ShapeDtypeStruct((rows, ATT_WIDTH), F32),
        jax.ShapeDtypeStruct((rows, ATT_WIDTH), F32),
        jax.ShapeDtypeStruct((rows, ATT_WIDTH), F32),
        jax.ShapeDtypeStruct((rows, SSM_WIDTH), F32),
        jax.ShapeDtypeStruct((rows, CONV_DIM), F32),
        jax.ShapeDtypeStruct((rows, LANES), F32),
    ]
    out_specs = [row(s.shape[1]) for s in out_shapes]
    if attn_tk is not None:
        per_tile = attn_tk // tm
        assert per_tile * tm == attn_tk and rows % attn_tk == 0
        out_shapes += [jax.ShapeDtypeStruct((ATT_HEADS, rows, V_DIM), BF16),
                       jax.ShapeDtypeStruct((rows // attn_tk, ATT_WIDTH, attn_tk), BF16)]
        out_specs[0] = pl.BlockSpec((ATT_HEADS, tm, V_DIM), lambda i: (0, i, 0))
        out_specs += [pl.BlockSpec((ATT_HEADS, tm, V_DIM), lambda i: (0, i, 0)),
                      pl.BlockSpec((None, ATT_WIDTH, tm), lambda i: (i // per_tile, 0, i % per_tile))]
    return pl.pallas_call(
        _inproj_body,
        grid=(rows // tm,),
        in_specs=[row(D_MODEL), wspec(w["norm"]), wspec(w["q"]), wspec(w["k"]), wspec(w["v"]),
                  wspec(w["g"]), wspec(w["z"]), wspec(w["xbc"]), wspec(w["dt"]),
                  row(LANES), row(LANES)],
        out_specs=out_specs,
        out_shape=out_shapes,
        compiler_params=_cparams(1),
        name="inproj",
    )(x, w["norm"], w["q"], w["k"], w["v"], w["g"], w["z"], w["xbc"], w["dt"], cos, sin)


def _lambda_value(lq1_ref, lk1_ref, lq2_ref, lk2_ref):
    a = jnp.sum(lq1_ref[...] * lk1_ref[...], axis=-1, keepdims=True)
    b = jnp.sum(lq2_ref[...] * lk2_ref[...], axis=-1, keepdims=True)
    return jnp.exp(a) - jnp.exp(b) + LAM_INIT


def _sub_norm_gate(o, sub_w, g):
    o = o * lax.rsqrt(jnp.mean(o * o, axis=-1, keepdims=True) + EPS) * sub_w * (1.0 - LAM_INIT)
    return o * _silu(g)


def _prompt_attn_body(q_ref, k_ref, vt_ref, g_ref, sub_ref, lq1_ref, lk1_ref, lq2_ref, lk2_ref,
                      o_ref, m_ref, l_ref, acc_ref, sa_ref, sb_ref, mxa_ref, mxb_ref, bias_ref, *, tq, tk):
    i = pl.program_id(1)

    @pl.when((pl.program_id(0) == 0) & (i == 0))
    def _():
        r = lax.broadcasted_iota(jnp.int32, bias_ref.shape, 0)
        c = lax.broadcasted_iota(jnp.int32, bias_ref.shape, 1) % tq
        bias_ref[...] = jnp.where(r <= c, 0.0, NEG_BIG)

    qt = jnp.transpose(q_ref[...].astype(F32)).astype(BF16)
    zero = jnp.zeros((QK_DIM, tq), BF16)
    q_blk = jnp.concatenate([jnp.concatenate([qt[:QK_DIM], zero], axis=0),
                             jnp.concatenate([zero, qt[QK_DIM:]], axis=0)], axis=1)
    m_ref[...] = jnp.full(m_ref.shape, NEG_BIG, F32)
    l_ref[...] = jnp.zeros(l_ref.shape, F32)
    acc_ref[...] = jnp.zeros(acc_ref.shape, F32)

    def scores(t, s_ref, mx_ref):
        kt = k_ref[pl.ds(pl.multiple_of(t * tk, tk), tk), :]
        s = jnp.dot(kt, q_blk, preferred_element_type=F32)
        s_ref[...] = s
        mx_ref[...] = jnp.max(s, axis=0, keepdims=True)

    def update(t, s_ref, mx_ref, diagonal):
        s = s_ref[...]
        if diagonal:
            s = s + bias_ref[...]
            mx = jnp.max(s, axis=0, keepdims=True)
        else:
            mx = mx_ref[...]
        m_prev = m_ref[...]
        m_new = jnp.maximum(m_prev, mx)
        alpha = jnp.exp2(m_prev - m_new)
        p = jnp.exp2(s - m_new).astype(BF16)
        v_ext = jnp.concatenate([vt_ref[t], jnp.ones((BF16_ROWS, tk), BF16)], axis=0)
        pv = jnp.dot(v_ext, p, preferred_element_type=F32)
        l_ref[...] = alpha * l_ref[...] + pv[V_DIM:V_DIM + 1]
        acc_ref[...] = alpha * acc_ref[...] + pv[:V_DIM]
        m_ref[...] = m_new

    slots = ((sa_ref, mxa_ref), (sb_ref, mxb_ref))

    def run(t0, count):
        for k in range(count):
            scores(t0 + k + 1, *slots[(k + 1) % 2])
            update(t0 + k, *slots[k % 2], False)

    n_full = (i * tq) // tk
    scores(0, *slots[0])

    def unrolled(j, carry):
        run(UNROLL * j, UNROLL)
        return carry

    lax.fori_loop(0, n_full // UNROLL, unrolled, 0)
    t_rem = (n_full // UNROLL) * UNROLL
    for rem in range(UNROLL):
        @pl.when(n_full % UNROLL == rem)
        def _(rem=rem):
            run(t_rem, rem)
            update(n_full, *slots[rem % 2], True)

    o_both = acc_ref[...] / l_ref[...]
    lam = _lambda_value(lq1_ref, lk1_ref, lq2_ref, lk2_ref)
    o = jnp.transpose(o_both[:, :tq] - lam * o_both[:, tq:])
    o_ref[...] = _sub_norm_gate(o, sub_ref[...], g_ref[...]).astype(o_ref.dtype)


def _prompt_attn(q, k, vt, g, sub_w, lams, tq, tk):
    seq = q.shape[1]
    assert tq == tk and seq % tq == 0 and vt.shape == (seq // tk, ATT_WIDTH, tk)
    head_tile = pl.BlockSpec((tq, V_DIM), lambda h, i: (i, h))
    q_tile = pl.BlockSpec((None, tq, V_DIM), lambda h, i: (h, i, 0))
    k_all = pl.BlockSpec((None, seq, V_DIM), lambda h, i: (h, 0, 0))
    vt_all = pl.BlockSpec((seq // tk, V_DIM, tk), lambda h, i: (0, h, 0))
    small = lambda a: pl.BlockSpec(a.shape, lambda h, i: (0, 0))
    return pl.pallas_call(
        functools.partial(_prompt_attn_body, tq=tq, tk=tk),
        grid=(ATT_HEADS, seq // tq),
        in_specs=[q_tile, k_all, vt_all, head_tile, small(sub_w)] + [small(a) for a in lams],
        out_specs=head_tile,
        out_shape=jax.ShapeDtypeStruct((seq, ATT_WIDTH), BF16),
        scratch_shapes=[pltpu.VMEM((1, 2 * tq), F32), pltpu.VMEM((1, 2 * tq), F32),
                        pltpu.VMEM((V_DIM, 2 * tq), F32),
                        pltpu.VMEM((tk, 2 * tq), F32), pltpu.VMEM((tk, 2 * tq), F32),
                        pltpu.VMEM((1, 2 * tq), F32), pltpu.VMEM((1, 2 * tq), F32),
                        pltpu.VMEM((tk, 2 * tq), F32)],
        compiler_params=_cparams(2),
        name="prompt_attn",
    )(q, k, vt, g, sub_w, *lams)


XROWS = ATT_HEADS * SUBLANES


def _decode_attn_body(pt_ref, *refs, n_pages, dec_seq, group):
    del pt_ref
    q_ref = refs[0]
    k_refs = refs[1:1 + group]
    v_refs = refs[1 + group:1 + 2 * group]
    (kn_ref, vn_ref, g_ref, sub_ref, lq1_ref, lk1_ref, lq2_ref, lk2_ref,
     o_ref, m_ref, l_ref, acc_ref, qblk_ref, kbuf_ref, vbuf_ref) = refs[1 + 2 * group:]
    step = pl.program_id(1)
    page = LANES

    @pl.when(step == 0)
    def _():
        m_ref[...] = jnp.full(m_ref.shape, NEG_BIG, F32)
        l_ref[...] = jnp.zeros(l_ref.shape, F32)
        acc_ref[...] = jnp.zeros(acc_ref.shape, F32)
        q8 = q_ref[...]
        row = lax.broadcasted_iota(jnp.int32, q8.shape, 0)
        lane = lax.broadcasted_iota(jnp.int32, q8.shape, 1)
        own_map = ((lane % V_DIM) < QK_DIM) == (row < dec_seq)
        for h in range(ATT_HEADS):
            keep = own_map & ((lane // V_DIM) == h)
            qblk_ref[h * SUBLANES:(h + 1) * SUBLANES, :] = jnp.where(keep, q8, 0.0).astype(BF16)

    qblk = qblk_ref[...]
    first_of_pair = lax.broadcasted_iota(jnp.int32, (BF16_ROWS, V_DIM), 0) < SUBLANES

    def process(s, value_fn):
        m_prev = m_ref[...]
        m_new = jnp.maximum(m_prev, jnp.max(s, axis=-1, keepdims=True))
        alpha = jnp.exp2(m_prev - m_new)
        p = jnp.exp2(s - m_new)
        l_ref[...] = alpha * l_ref[...] + jnp.sum(p, axis=-1, keepdims=True)
        p = p.astype(BF16)
        pv_parts = []
        for pair in range(ATT_HEADS // 2):
            p_pair = p[pair * BF16_ROWS:(pair + 1) * BF16_ROWS]
            halves = [jnp.dot(p_pair, value_fn(h).astype(BF16), preferred_element_type=F32)
                      for h in (2 * pair, 2 * pair + 1)]
            pv_parts.append(jnp.where(first_of_pair, halves[0], halves[1]))
        acc_ref[...] = alpha * acc_ref[...] + jnp.concatenate(pv_parts, axis=0)
        m_ref[...] = m_new

    s_pages = jnp.concatenate([jnp.dot(qblk, kp[...].astype(BF16), preferred_element_type=F32)
                               for kp in k_refs], axis=1)
    process(s_pages, lambda h: jnp.concatenate(
        [vp[pl.ds(h, page, stride=ATT_HEADS), :] for vp in v_refs], axis=0))

    @pl.when(step == n_pages // group - 1)
    def _():
        kbuf_ref[...] = jnp.zeros(kbuf_ref.shape, F32)
        vbuf_ref[...] = jnp.zeros(vbuf_ref.shape, F32)
        kbuf_ref[0:SUBLANES, :] = kn_ref[...]
        vbuf_ref[0:SUBLANES, :] = vn_ref[...]
        s_new = lax.dot_general(qblk, kbuf_ref[...].astype(BF16), (((1,), (1,)), ((), ())),
                                preferred_element_type=F32)
        qi = lax.broadcasted_iota(jnp.int32, s_new.shape, 0) % dec_seq
        kj = lax.broadcasted_iota(jnp.int32, s_new.shape, 1)
        process(jnp.where(kj <= qi, s_new, NEG_BIG), lambda h: vbuf_ref[:, h * V_DIM:(h + 1) * V_DIM])

        t = acc_ref[...] / l_ref[...]
        lam = _lambda_value(lq1_ref, lk1_ref, lq2_ref, lk2_ref)
        g = g_ref[...]
        for h in range(ATT_HEADS):
            cols = slice(h * V_DIM, (h + 1) * V_DIM)
            th = t[h * SUBLANES:(h + 1) * SUBLANES]
            o = th - lam * pltpu.roll(th, SUBLANES - dec_seq, 0)
            o_ref[:, cols] = _sub_norm_gate(o, sub_ref[...], g[:, cols])


def _decode_attn(page_table, q8, cache_kt, cache_vr, k_new8, v_new8, g8, sub_w, lams, dec_seq, group):
    batch, n_pages = page_table.shape
    page = cache_kt.shape[2]
    assert page == LANES and 2 * dec_seq == SUBLANES and n_pages % group == 0
    per_b = lambda a: pl.BlockSpec((None,) + a.shape[1:], lambda b, p, pt: (b, 0, 0))
    paged = lambda a, n: pl.BlockSpec((None,) + a.shape[1:],
                                      lambda b, p, pt: (pt[b, p * group + n], 0, 0))
    small = lambda a: pl.BlockSpec(a.shape, lambda b, p, pt: (0, 0))
    grid_spec = pltpu.PrefetchScalarGridSpec(
        num_scalar_prefetch=1,
        grid=(batch, n_pages // group),
        in_specs=[per_b(q8)] + [paged(cache_kt, n) for n in range(group)]
                 + [paged(cache_vr, n) for n in range(group)]
                 + [per_b(k_new8), per_b(v_new8), per_b(g8), small(sub_w)]
                 + [small(a) for a in lams],
        out_specs=per_b(g8),
        scratch_shapes=[pltpu.VMEM((XROWS, 1), F32), pltpu.VMEM((XROWS, 1), F32),
                        pltpu.VMEM((XROWS, V_DIM), F32), pltpu.VMEM((XROWS, ATT_WIDTH), BF16),
                        pltpu.VMEM((page, ATT_WIDTH), F32), pltpu.VMEM((page, ATT_WIDTH), F32)],
    )
    return pl.pallas_call(
        functools.partial(_decode_attn_body, n_pages=n_pages, dec_seq=dec_seq, group=group),
        grid_spec=grid_spec,
        out_shape=jax.ShapeDtypeStruct(g8.shape, F32),
        compiler_params=_cparams(2),
        name="decode_attn",
    )(page_table, q8, *([cache_kt] * group), *([cache_vr] * group), k_new8, v_new8, g8, sub_w, *lams)


HEADS_PER_GROUP = SSM_HEADS // SSM_GROUPS
GROUP_WIDTH = SSM_WIDTH // SSM_GROUPS
TAIL = SUBLANES


def _pair_expand(a, j):
    lane = lax.broadcasted_iota(jnp.int32, (a.shape[0], LANES), 1)
    return jnp.where(lane < SSM_HEAD_DIM, a[:, 2 * j:2 * j + 1], a[:, 2 * j + 1:2 * j + 2])


def _ssd_body(xbc_ref, dt_ref, z_ref, cprev_ref, h0_ref, cw_ref, cb_ref, dtb_ref, alog_ref,
              dskip_ref, gn_ref,
              y_ref, cnew_ref, hnew_ref,
              xs_ref, dtp_ref, ht_ref, *, valid, n_chunks):
    q = CHUNK
    c = pl.program_id(1)
    first_step = (pl.program_id(0) == 0) & (c == 0)

    @pl.when(first_step)
    def _():
        xs_ref[...] = jnp.zeros(xs_ref.shape, F32)
        dtp_ref[...] = jnp.zeros(dtp_ref.shape, F32)

    @pl.when(c == 0)
    def _():
        xs_ref[TAIL - (CONV_W - 1):TAIL, :] = cprev_ref[...]
        ht_ref[...] = jnp.transpose(h0_ref[...])

    xs_ref[TAIL:TAIL + valid, :] = xbc_ref[...]
    dtp_ref[0:valid, :] = dt_ref[...]

    u = cb_ref[...]
    for w in range(CONV_W):
        u = u + xs_ref[pl.ds(TAIL - (CONV_W - 1) + w, q), :] * cw_ref[w:w + 1, :]
    u = _silu(u)

    @pl.when(c == n_chunks - 1)
    def _():
        cnew_ref[...] = xs_ref[pl.ds(TAIL + valid - (CONV_W - 1), CONV_W - 1), :]

    if n_chunks > 1:
        xs_ref[0:TAIL, :] = xs_ref[q:q + TAIL, :]

    xs = u[:, :SSM_WIDTH]
    bm = u[:, SSM_WIDTH:SSM_WIDTH + SSM_GROUPS * D_STATE]
    cm = u[:, SSM_WIDTH + SSM_GROUPS * D_STATE:]

    row = lax.broadcasted_iota(jnp.int32, (q, LANES), 0)
    x_dt = dtp_ref[...] + dtb_ref[...]
    dt = jnp.maximum(x_dt, 0.0) + jnp.log1p(jnp.exp(-jnp.abs(x_dt)))
    if valid < q:
        dt = jnp.where(row < valid, dt, 0.0)
    a = -jnp.exp(alog_ref[...])
    da = dt * a

    tt = lax.broadcasted_iota(jnp.int32, (q, q), 0)
    ss = lax.broadcasted_iota(jnp.int32, (q, q), 1)
    causal = ss <= tt
    tril = jnp.where(causal, 1.0, 0.0).astype(BF16)
    da_hi = da.astype(BF16)
    r1 = da - da_hi.astype(F32)
    da_mid = r1.astype(BF16)
    da_lo = (r1 - da_mid.astype(F32)).astype(BF16)
    cs3 = jnp.dot(tril, jnp.concatenate([da_hi, da_mid, da_lo], axis=1), preferred_element_type=F32)
    cs = cs3[:, :LANES] + cs3[:, LANES:2 * LANES] + cs3[:, 2 * LANES:]

    cs_t = jnp.transpose(cs)
    dt_t = jnp.transpose(dt)
    cs_last = cs[q - 1:q, :]
    e_cs = jnp.exp(cs)
    dec_end = jnp.exp(cs_last - cs) * dt
    chunk_decay = jnp.exp(cs_last)

    ht = ht_ref[...]
    ht_b = ht.astype(BF16)
    y_parts = []
    st_parts = []
    for grp in range(SSM_GROUPS):
        bg = bm[:, grp * D_STATE:(grp + 1) * D_STATE]
        cg = cm[:, grp * D_STATE:(grp + 1) * D_STATE].astype(BF16)
        cb = lax.dot_general(cg, bg.astype(BF16), (((1,), (1,)), ((), ())),
                             preferred_element_type=F32)
        y_off = jnp.dot(cg, ht_b[:, grp * GROUP_WIDTH:(grp + 1) * GROUP_WIDTH],
                        preferred_element_type=F32)
        xd_parts = []
        for jp in range(HEADS_PER_GROUP // 2):
            j = grp * (HEADS_PER_GROUP // 2) + jp
            x_pair = xs[:, j * LANES:(j + 1) * LANES]
            x_pair_b = x_pair.astype(BF16)
            y_pair = []
            for h in (2 * j, 2 * j + 1):
                diff = cs[:, h:h + 1] - cs_t[h:h + 1, :]
                lmat = jnp.exp(jnp.where(causal, diff, NEG_BIG))
                wmat = (cb * lmat * dt_t[h:h + 1, :]).astype(BF16)
                y_pair.append(jnp.dot(wmat, x_pair_b, preferred_element_type=F32))
            lane = lax.broadcasted_iota(jnp.int32, (q, LANES), 1)
            y_diag = jnp.where(lane < SSM_HEAD_DIM, y_pair[0], y_pair[1])
            y_parts.append(y_diag + y_off[:, jp * LANES:(jp + 1) * LANES] * _pair_expand(e_cs, j))
            xd_parts.append((x_pair * _pair_expand(dec_end, j)).astype(BF16))
        xd = jnp.concatenate(xd_parts, axis=1)
        st_parts.append(jnp.dot(jnp.transpose(bg).astype(BF16), xd, preferred_element_type=F32))
    decay_lanes = jnp.concatenate([_pair_expand(chunk_decay, j) for j in range(SSM_HEADS // 2)], axis=1)
    ht_new = ht * decay_lanes + jnp.concatenate(st_parts, axis=1)
    ht_ref[...] = ht_new

    @pl.when(c == n_chunks - 1)
    def _():
        hnew_ref[...] = jnp.transpose(ht_new)

    y = jnp.concatenate(y_parts, axis=1) + dskip_ref[...] * xs
    y = y[0:valid] * _silu(z_ref[...])
    outs = []
    for grp in range(SSM_GROUPS):
        yg = y[:, grp * GROUP_WIDTH:(grp + 1) * GROUP_WIDTH]
        outs.append(yg * lax.rsqrt(jnp.mean(yg * yg, axis=-1, keepdims=True) + EPS))
    y_ref[...] = (jnp.concatenate(outs, axis=1) * gn_ref[...]).astype(y_ref.dtype)


def _ssd(xbc, dt_raw, z, conv_prev, h0, w):
    n_seq, rows, _ = xbc.shape
    valid = min(rows, CHUNK)
    n_chunks = rows // valid
    assert n_chunks * valid == rows and valid >= CONV_W - 1
    blk = lambda n: pl.BlockSpec((None, valid, n), lambda s, c: (s, c, 0))
    per_seq = lambda a: pl.BlockSpec((None,) + a.shape[1:], lambda s, c: (s, 0, 0))
    small = lambda a: pl.BlockSpec(a.shape, lambda s, c: (0, 0))
    consts = [w["conv_w"], w["conv_b"], w["dt_bias"], w["a_log"], w["d_skip"], w["gnorm"]]
    return pl.pallas_call(
        functools.partial(_ssd_body, valid=valid, n_chunks=n_chunks),
        grid=(n_seq, n_chunks),
        in_specs=[blk(CONV_DIM), blk(LANES), blk(SSM_WIDTH), per_seq(conv_prev), per_seq(h0)]
                 + [small(a) for a in consts],
        out_specs=[blk(SSM_WIDTH), per_seq(conv_prev), per_seq(h0)],
        out_shape=(jax.ShapeDtypeStruct((n_seq, rows, SSM_WIDTH), BF16),
                   jax.ShapeDtypeStruct(conv_prev.shape, F32),
                   jax.ShapeDtypeStruct(h0.shape, F32)),
        scratch_shapes=[pltpu.VMEM((CHUNK + TAIL, CONV_DIM), F32),
                        pltpu.VMEM((CHUNK, LANES), F32),
                        pltpu.VMEM((D_STATE, SSM_WIDTH), F32)],
        compiler_params=_cparams(2),
        name="ssd",
    )(xbc, dt_raw, z, conv_prev, h0, *consts)


def _outproj_body(att_ref, ssm_ref, x_ref, p_ref, woa_ref, wos_ref, wpp_ref, wpg_ref, fn_ref, y_ref):
    hmid = (x_ref[...]
            + jnp.dot(att_ref[...], woa_ref[...], preferred_element_type=F32)
            + jnp.dot(ssm_ref[...], wos_ref[...], preferred_element_type=F32))
    emb = jnp.dot(p_ref[...].astype(BF16), wpp_ref[...], preferred_element_type=F32)
    gate = jax.nn.sigmoid(jnp.dot(hmid.astype(BF16), wpg_ref[...], preferred_element_type=F32))
    out = hmid + emb * gate
    y_ref[...] = out * lax.rsqrt(jnp.mean(out * out, axis=-1, keepdims=True) + EPS) * fn_ref[...]


def _outproj(att, ssm, x, p, w, tm):
    rows = x.shape[0]
    row = lambda n: pl.BlockSpec((tm, n), lambda i: (i, 0))
    wspec = lambda a: pl.BlockSpec(a.shape, lambda i: (0, 0))
    ws = [w["out_att"], w["out_ssm"], w["ple_proj"], w["ple_gate"], w["final_norm"]]
    return pl.pallas_call(
        _outproj_body,
        grid=(rows // tm,),
        in_specs=[row(ATT_WIDTH), row(SSM_WIDTH), row(D_MODEL), row(PLE_DIM)] + [wspec(a) for a in ws],
        out_specs=row(D_MODEL),
        out_shape=jax.ShapeDtypeStruct((rows, D_MODEL), F32),
        compiler_params=_cparams(1),
        name="outproj",
    )(att, ssm, x, p, *ws)


def _rope_tables(pos):
    half = QK_DIM // 2
    inv = ROPE_THETA ** (-jnp.arange(0, QK_DIM, 2, dtype=F32) / QK_DIM)
    ang = pos.astype(F32)[:, None] * inv[None, :]
    cos = jnp.tile(jnp.cos(ang), (1, LANES // half))
    sin = jnp.sin(ang)
    sin = jnp.tile(jnp.concatenate([-sin, sin], axis=1), (1, LANES // QK_DIM))
    return cos, sin


def _pad_to(a, axis, size):
    pads = [(0, 0)] * a.ndim
    pads[axis] = (0, size - a.shape[axis])
    return jnp.pad(a, pads)


def kernel(x_prompt, x_sample, cache_k, cache_v, state_conv, state_ssm, page_table, p_prompt, p_sample, w_norm, w_in, lambda_q1, lambda_k1, lambda_q2, lambda_k2, subln_w, conv_w, conv_b, dt_bias, A_log, D_skip, gnorm_w, w_out, w_ple_proj, w_ple_gate, final_norm_w):
    assert w_norm.shape[0] == 1 and x_prompt.shape[0] == 1
    seq = x_prompt.shape[1]
    dec_batch, dec_seq, _ = x_sample.shape
    n_pool, page = cache_k.shape[1], cache_k.shape[2]
    past = page_table.shape[1] * page

    w_in0 = w_in[0]
    splits = [0, 1024, 2048, 3072, 4096, 5120, 5120 + CONV_DIM, 5120 + CONV_DIM + SSM_HEADS]
    names = ["q", "k", "v", "g", "z", "xbc", "dt"]
    w = {n: w_in0[:, a:b].astype(BF16) for n, a, b in zip(names, splits[:-1], splits[1:])}
    w["dt"] = _pad_to(w["dt"], 1, LANES)
    w["norm"] = w_norm
    ssd_w = {"conv_w": conv_w[0], "conv_b": conv_b, "dt_bias": _pad_to(dt_bias, 1, LANES),
             "a_log": _pad_to(A_log, 1, LANES), "d_skip": jnp.repeat(D_skip, SSM_HEAD_DIM, axis=1),
             "gnorm": gnorm_w}
    out_w = {"out_att": w_out[0, :ATT_WIDTH].astype(BF16), "out_ssm": w_out[0, ATT_WIDTH:].astype(BF16),
             "ple_proj": w_ple_proj[0].astype(BF16), "ple_gate": w_ple_gate[0].astype(BF16),
             "final_norm": final_norm_w[None, :]}
    lams = [lambda_q1, lambda_k1, lambda_q2, lambda_k2]

    cos_p, sin_p = _rope_tables(jnp.arange(seq))
    q_p, k_p, v_p, g_p, z_p, xbc_p, dt_p, kb_p, vt_p = _inproj(
        x_prompt[0], w, cos_p, sin_p, tm=INPROJ_TM, attn_tk=ATTN_TK)
    att_p = _prompt_attn(q_p, kb_p, vt_p, g_p, subln_w, lams, tq=ATTN_TQ, tk=ATTN_TK)
    ssm_p, conv_p, h_p = _ssd(xbc_p[None], dt_p[None], z_p[None],
                              jnp.zeros((1, CONV_W - 1, CONV_DIM), F32),
                              jnp.zeros((1, SSM_WIDTH, D_STATE), F32), ssd_w)
    y_p = _outproj(att_p, ssm_p[0], x_prompt[0], p_prompt[0, 0], out_w, tm=OUTPROJ_TM)

    n_rows = dec_batch * dec_seq
    cos_s, sin_s = _rope_tables(past + jnp.arange(n_rows) % dec_seq)
    q_s, k_s, v_s, g_s, z_s, xbc_s, dt_s = _inproj(x_sample.reshape(n_rows, D_MODEL), w,
                                                 cos_s, sin_s, tm=n_rows)
    per_seq = lambda a: a.reshape(dec_batch, dec_seq, a.shape[-1])
    q3 = per_seq(q_s)
    cache_kt = jnp.transpose(cache_k[0], (0, 2, 3, 4, 1)).reshape(n_pool, ATT_WIDTH, page)
    cache_vr = cache_v[0].reshape(n_pool, page * ATT_HEADS, V_DIM)
    pad8 = lambda a: _pad_to(per_seq(a), 1, SUBLANES)
    att_s = _decode_attn(page_table, jnp.concatenate([q3, q3], axis=1).astype(F32), cache_kt, cache_vr,
                         pad8(k_s), pad8(v_s), pad8(g_s), subln_w, lams, dec_seq, DECODE_PAGES)
    att_s = att_s[:, :dec_seq].reshape(n_rows, ATT_WIDTH).astype(BF16)
    ssm_s, conv_s, h_s = _ssd(per_seq(xbc_s), per_seq(dt_s), per_seq(z_s), state_conv[0],
                              state_ssm[0].reshape(dec_batch, SSM_WIDTH, D_STATE), ssd_w)
    y_s = _outproj(att_s, ssm_s.reshape(n_rows, SSM_WIDTH), x_sample.reshape(n_rows, D_MODEL),
                   p_sample[0].reshape(n_rows, PLE_DIM), out_w, tm=n_rows)

    hp = (SSM_HEADS, SSM_HEAD_DIM, D_STATE)
    return (y_p[None],
            y_s.reshape(dec_batch, dec_seq, D_MODEL),
            k_p.reshape(1, 1, seq, ATT_HEADS, 2, QK_DIM),
            v_p.reshape(1, 1, seq, ATT_HEADS, V_DIM),
            conv_p[None],
            h_p.reshape((1, 1) + hp),
            k_s.reshape(1, dec_batch, dec_seq, ATT_HEADS, 2, QK_DIM),
            v_s.reshape(1, dec_batch, dec_seq, ATT_HEADS, V_DIM),
            conv_s[None],
            h_s.reshape((1, dec_batch) + hp))
```

```python
import functools
import math

import jax
import jax.numpy as jnp
from jax import lax
from jax.experimental import pallas as pl
from jax.experimental.pallas import tpu as pltpu

F32 = jnp.float32
BF16 = jnp.bfloat16

D_MODEL = 1024
ATT_HEADS = 8
QK_DIM = 64
V_DIM = 2 * QK_DIM
ATT_WIDTH = ATT_HEADS * V_DIM
SM_SCALE = QK_DIM ** -0.5
LOG2E = math.log2(math.e)
ROPE_THETA = 10000.0
SSM_WIDTH = 1024
SSM_HEAD_DIM = 64
SSM_HEADS = SSM_WIDTH // SSM_HEAD_DIM
SSM_GROUPS = 2
D_STATE = 128
CONV_W = 4
CONV_DIM = SSM_WIDTH + 2 * SSM_GROUPS * D_STATE
CHUNK = 128
PLE_DIM = 256
EPS = 1e-6
LAM_INIT = 0.8 - 0.6 * math.exp(-0.3 * 0)

LANES = 128
SUBLANES = 8
BF16_ROWS = 16
NEG_BIG = -1e30
VMEM_LIMIT = 56 * 1024 * 1024

INPROJ_TM = 256
OUTPROJ_TM = 512
ATTN_TQ = 512
ATTN_TK = 512
UNROLL = 2
HEADS_PER_STEP = 2
DECODE_PAGES = 16


def _cparams(n_axes):
    return pltpu.CompilerParams(dimension_semantics=("arbitrary",) * n_axes,
                                vmem_limit_bytes=VMEM_LIMIT)


def _silu(x):
    return x * jax.nn.sigmoid(x)


def _inproj_body(x_ref, wn_ref, wq_ref, wk_ref, wv_ref, wg_ref, wz_ref, wx_ref, wdt_ref,
                 cos_ref, sin_ref, q_ref, k_ref, v_ref, g_ref, z_ref, xbc_ref, dt_ref,
                 *attn_copies):
    x = x_ref[...]
    hn = x * lax.rsqrt(jnp.mean(x * x, axis=-1, keepdims=True) + EPS) * wn_ref[...]
    hn = hn.astype(BF16)
    cos = cos_ref[...]
    sin = sin_ref[...]
    lane = lax.broadcasted_iota(jnp.int32, cos.shape, 1)
    first_half = (lane % QK_DIM) < (QK_DIM // 2)

    def rope_chunks(t):
        for c in range(t.shape[1] // LANES):
            tc = t[:, c * LANES:(c + 1) * LANES]
            partner = jnp.where(first_half,
                                pltpu.roll(tc, LANES - QK_DIM // 2, 1),
                                pltpu.roll(tc, QK_DIM // 2, 1))
            yield c, tc * cos + partner * sin

    q = jnp.dot(hn, wq_ref[...], preferred_element_type=F32)
    for c, r in rope_chunks(q):
        r = r * (SM_SCALE * LOG2E)
        if attn_copies:
            q_ref[c] = r.astype(BF16)
        else:
            q_ref[:, c * LANES:(c + 1) * LANES] = r.astype(BF16)
    k = jnp.dot(hn, wk_ref[...], preferred_element_type=F32)
    for c, r in rope_chunks(k):
        k_ref[:, c * LANES:(c + 1) * LANES] = r
        if attn_copies:
            attn_copies[0][c] = r.astype(BF16)
    v = jnp.dot(hn, wv_ref[...], preferred_element_type=F32)
    v_ref[...] = v
    if attn_copies:
        attn_copies[1][...] = jnp.transpose(v).astype(BF16)
    g_ref[...] = jnp.dot(hn, wg_ref[...], preferred_element_type=F32)
    z_ref[...] = jnp.dot(hn, wz_ref[...], preferred_element_type=F32)
    xbc_ref[...] = jnp.dot(hn, wx_ref[...], preferred_element_type=F32)
    dt_ref[...] = jnp.dot(hn, wdt_ref[...], preferred_element_type=F32)


def _inproj(x, w, cos, sin, tm, attn_tk=None):
    rows = x.shape[0]
    row = lambda n: pl.BlockSpec((tm, n), lambda i: (i, 0))
    wspec = lambda a: pl.BlockSpec(a.shape, lambda i: (0, 0), pipeline_mode=pl.Buffered(1))
    q_shape = (ATT_HEADS, rows, V_DIM) if attn_tk is not None else (rows, ATT_WIDTH)
    out_shapes = [
        jax.ShapeDtypeStruct(q_shape, BF16),
        jax.ShapeDtypeStruct((rows, ATT_WIDTH), F32),
        jax.ShapeDtypeStruct((rows, ATT_WIDTH), F32),
        jax.ShapeDtypeStruct((rows, ATT_WIDTH), F32),
        jax.ShapeDtypeStruct((rows, SSM_WIDTH), F32),
        jax.ShapeDtypeStruct((rows, CONV_DIM), F32),
        jax.ShapeDtypeStruct((rows, LANES), F32),
    ]
    out_specs = [row(s.shape[1]) for s in out_shapes]
    if attn_tk is not None:
        per_tile = attn_tk // tm
        assert per_tile * tm == attn_tk and rows % attn_tk == 0
        out_shapes += [jax.ShapeDtypeStruct((ATT_HEADS, rows, V_DIM), BF16),
                       jax.ShapeDtypeStruct((rows // attn_tk, ATT_WIDTH, attn_tk), BF16)]
        out_specs[0] = pl.BlockSpec((ATT_HEADS, tm, V_DIM), lambda i: (0, i, 0))
        out_specs += [pl.BlockSpec((ATT_HEADS, tm, V_DIM), lambda i: (0, i, 0)),
                      pl.BlockSpec((None, ATT_WIDTH, tm), lambda i: (i // per_tile, 0, i % per_tile))]
    return pl.pallas_call(
        _inproj_body,
        grid=(rows // tm,),
        in_specs=[row(D_MODEL), wspec(w["norm"]), wspec(w["q"]), wspec(w["k"]), wspec(w["v"]),
                  wspec(w["g"]), wspec(w["z"]), wspec(w["xbc"]), wspec(w["dt"]),
                  row(LANES), row(LANES)],
        out_specs=out_specs,
        out_shape=out_shapes,
        compiler_params=_cparams(1),
        name="inproj",
    )(x, w["norm"], w["q"], w["k"], w["v"], w["g"], w["z"], w["xbc"], w["dt"], cos, sin)


def _lambda_value(lq1_ref, lk1_ref, lq2_ref, lk2_ref):
    a = jnp.sum(lq1_ref[...] * lk1_ref[...], axis=-1, keepdims=True)
    b = jnp.sum(lq2_ref[...] * lk2_ref[...], axis=-1, keepdims=True)
    return jnp.exp(a) - jnp.exp(b) + LAM_INIT


def _sub_norm_gate(o, sub_w, g):
    o = o * lax.rsqrt(jnp.mean(o * o, axis=-1, keepdims=True) + EPS) * sub_w * (1.0 - LAM_INIT)
    return o * _silu(g)


def _prompt_attn_body(q_ref, k_ref, vt_ref, g_ref, sub_ref, lq1_ref, lk1_ref, lq2_ref, lk2_ref,
                      o_ref, m_ref, l_ref, acc_ref, s_ref, mx_ref, bias_ref, *, tq, tk):
    i = pl.program_id(1)

    @pl.when((pl.program_id(0) == 0) & (i == 0))
    def _():
        r = lax.broadcasted_iota(jnp.int32, bias_ref.shape, 0)
        c = lax.broadcasted_iota(jnp.int32, bias_ref.shape, 1) % tq
        bias_ref[...] = jnp.where(r <= c, 0.0, NEG_BIG)

    m_ref[...] = jnp.full(m_ref.shape, NEG_BIG, F32)
    l_ref[...] = jnp.zeros(l_ref.shape, F32)
    acc_ref[...] = jnp.zeros(acc_ref.shape, F32)

    zero = jnp.zeros((QK_DIM, tq), BF16)
    q_blk = []
    for hh in range(HEADS_PER_STEP):
        qt = jnp.transpose(q_ref[hh].astype(F32)).astype(BF16)
        q_blk.append(jnp.concatenate([jnp.concatenate([qt[:QK_DIM], zero], axis=0),
                                      jnp.concatenate([zero, qt[QK_DIM:]], axis=0)], axis=1))

    def scores(hh, t):
        kt = k_ref[hh, pl.ds(pl.multiple_of(t * tk, tk), tk), :]
        s = jnp.dot(kt, q_blk[hh], preferred_element_type=F32)
        s_ref[hh] = s
        mx_ref[hh] = jnp.max(s, axis=0, keepdims=True)

    def update(hh, t, diagonal):
        s = s_ref[hh]
        if diagonal:
            s = s + bias_ref[...]
            mx = jnp.max(s, axis=0, keepdims=True)
        else:
            mx = mx_ref[hh]
        m_prev = m_ref[hh]
        m_new = jnp.maximum(m_prev, mx)
        alpha = jnp.exp2(m_prev - m_new)
        p = jnp.exp2(s - m_new).astype(BF16)
        v_ext = jnp.concatenate([vt_ref[t, hh * V_DIM:(hh + 1) * V_DIM, :], jnp.ones((BF16_ROWS, tk), BF16)],
                                axis=0)
        pv = jnp.dot(v_ext, p, preferred_element_type=F32)
        l_ref[hh] = alpha * l_ref[hh] + pv[V_DIM:V_DIM + 1]
        acc_ref[hh] = alpha * acc_ref[hh] + pv[:V_DIM]
        m_ref[hh] = m_new

    def run(t0, count):
        for k in range(count):
            scores(1, t0 + k)
            update(0, t0 + k, False)
            scores(0, t0 + k + 1)
            update(1, t0 + k, False)

    n_full = (i * tq) // tk
    scores(0, 0)

    def unrolled(j, carry):
        run(UNROLL * j, UNROLL)
        return carry

    lax.fori_loop(0, n_full // UNROLL, unrolled, 0)
    t_rem = (n_full // UNROLL) * UNROLL
    for rem in range(UNROLL):
        @pl.when(n_full % UNROLL == rem)
        def _(rem=rem):
            run(t_rem, rem)
            scores(1, n_full)
            update(0, n_full, True)
            update(1, n_full, True)

    lam = _lambda_value(lq1_ref, lk1_ref, lq2_ref, lk2_ref)
    for hh in range(HEADS_PER_STEP):
        cols = slice(hh * V_DIM, (hh + 1) * V_DIM)
        o_both = acc_ref[hh] / l_ref[hh]
        o = jnp.transpose(o_both[:, :tq] - lam * o_both[:, tq:])
        o_ref[:, cols] = _sub_norm_gate(o, sub_ref[...], g_ref[:, cols]).astype(o_ref.dtype)


def _prompt_attn(q, k, vt, g, sub_w, lams, tq, tk):
    seq = q.shape[1]
    hps = HEADS_PER_STEP
    assert tq == tk and seq % tq == 0 and vt.shape == (seq // tk, ATT_WIDTH, tk) and ATT_HEADS % hps == 0
    row_tile = pl.BlockSpec((tq, hps * V_DIM), lambda h, i: (i, h))
    q_tile = pl.BlockSpec((hps, tq, V_DIM), lambda h, i: (h, i, 0))
    k_all = pl.BlockSpec((hps, seq, V_DIM), lambda h, i: (h, 0, 0))
    vt_all = pl.BlockSpec((seq // tk, hps * V_DIM, tk), lambda h, i: (0, h, 0))
    small = lambda a: pl.BlockSpec(a.shape, lambda h, i: (0, 0))
    return pl.pallas_call(
        functools.partial(_prompt_attn_body, tq=tq, tk=tk),
        grid=(ATT_HEADS // hps, seq // tq),
        in_specs=[q_tile, k_all, vt_all, row_tile, small(sub_w)] + [small(a) for a in lams],
        out_specs=row_tile,
        out_shape=jax.ShapeDtypeStruct((seq, ATT_WIDTH), BF16),
        scratch_shapes=[pltpu.VMEM((hps, 1, 2 * tq), F32), pltpu.VMEM((hps, 1, 2 * tq), F32),
                        pltpu.VMEM((hps, V_DIM, 2 * tq), F32),
                        pltpu.VMEM((hps, tk, 2 * tq), F32), pltpu.VMEM((hps, 1, 2 * tq), F32),
                        pltpu.VMEM((tk, 2 * tq), F32)],
        compiler_params=_cparams(2),
        name="prompt_attn",
    )(q, k, vt, g, sub_w, *lams)


XROWS = ATT_HEADS * SUBLANES


def _decode_attn_body(pt_ref, *refs, n_pages, dec_seq, group):
    del pt_ref
    q_ref = refs[0]
    k_refs = refs[1:1 + group]
    v_refs = refs[1 + group:1 + 2 * group]
    (kn_ref, vn_ref, g_ref, sub_ref, lq1_ref, lk1_ref, lq2_ref, lk2_ref,
     o_ref, m_ref, l_ref, acc_ref, qblk_ref, kbuf_ref, vbuf_ref) = refs[1 + 2 * group:]
    step = pl.program_id(1)
    page = LANES

    @pl.when(step == 0)
    def _():
        m_ref[...] = jnp.full(m_ref.shape, NEG_BIG, F32)
        l_ref[...] = jnp.zeros(l_ref.shape, F32)
        acc_ref[...] = jnp.zeros(acc_ref.shape, F32)
        q8 = q_ref[...]
        row = lax.broadcasted_iota(jnp.int32, q8.shape, 0)
        lane = lax.broadcasted_iota(jnp.int32, q8.shape, 1)
        own_map = ((lane % V_DIM) < QK_DIM) == (row < dec_seq)
        for h in range(ATT_HEADS):
            keep = own_map & ((lane // V_DIM) == h)
            qblk_ref[h * SUBLANES:(h + 1) * SUBLANES, :] = jnp.where(keep, q8, 0.0).astype(BF16)

    qblk = qblk_ref[...]
    first_of_pair = lax.broadcasted_iota(jnp.int32, (BF16_ROWS, V_DIM), 0) < SUBLANES

    def process(s, value_fn):
        m_prev = m_ref[...]
        m_new = jnp.maximum(m_prev, jnp.max(s, axis=-1, keepdims=True))
        alpha = jnp.exp2(m_prev - m_new)
        p = jnp.exp2(s - m_new)
        l_ref[...] = alpha * l_ref[...] + jnp.sum(p, axis=-1, keepdims=True)
        p = p.astype(BF16)
        pv_parts = []
        for pair in range(ATT_HEADS // 2):
            p_pair = p[pair * BF16_ROWS:(pair + 1) * BF16_ROWS]
            halves = [jnp.dot(p_pair, value_fn(h).astype(BF16), preferred_element_type=F32)
                      for h in (2 * pair, 2 * pair + 1)]
            pv_parts.append(jnp.where(first_of_pair, halves[0], halves[1]))
        acc_ref[...] = alpha * acc_ref[...] + jnp.concatenate(pv_parts, axis=0)
        m_ref[...] = m_new

    s_pages = jnp.concatenate([jnp.dot(qblk, kp[...].astype(BF16), preferred_element_type=F32)
                               for kp in k_refs], axis=1)
    process(s_pages, lambda h: jnp.concatenate(
        [vp[pl.ds(h, page, stride=ATT_HEADS), :] for vp in v_refs], axis=0))

    @pl.when(step == n_pages // group - 1)
    def _():
        kbuf_ref[...] = jnp.zeros(kbuf_ref.shape, F32)
        vbuf_ref[...] = jnp.zeros(vbuf_ref.shape, F32)
        kbuf_ref[0:SUBLANES, :] = kn_ref[...]
        vbuf_ref[0:SUBLANES, :] = vn_ref[...]
        s_new = lax.dot_general(qblk, kbuf_ref[...].astype(BF16), (((1,), (1,)), ((), ())),
                                preferred_element_type=F32)
        qi = lax.broadcasted_iota(jnp.int32, s_new.shape, 0) % dec_seq
        kj = lax.broadcasted_iota(jnp.int32, s_new.shape, 1)
        process(jnp.where(kj <= qi, s_new, NEG_BIG), lambda h: vbuf_ref[:, h * V_DIM:(h + 1) * V_DIM])

        t = acc_ref[...] / l_ref[...]
        lam = _lambda_value(lq1_ref, lk1_ref, lq2_ref, lk2_ref)
        g = g_ref[...]
        for h in range(ATT_HEADS):
            cols = slice(h * V_DIM, (h + 1) * V_DIM)
            th = t[h * SUBLANES:(h + 1) * SUBLANES]
            o = th - lam * pltpu.roll(th, SUBLANES - dec_seq, 0)
            o_ref[:, cols] = _sub_norm_gate(o, sub_ref[...], g[:, cols])


def _decode_attn(page_table, q8, cache_kt, cache_vr, k_new8, v_new8, g8, sub_w, lams, dec_seq, group):
    batch, n_pages = page_table.shape
    page = cache_kt.shape[2]
    assert page == LANES and 2 * dec_seq == SUBLANES and n_pages % group == 0
    per_b = lambda a: pl.BlockSpec((None,) + a.shape[1:], lambda b, p, pt: (b, 0, 0))
    paged = lambda a, n: pl.BlockSpec((None,) + a.shape[1:],
                                      lambda b, p, pt: (pt[b, p * group + n], 0, 0))
    small = lambda a: pl.BlockSpec(a.shape, lambda b, p, pt: (0, 0))
    grid_spec = pltpu.PrefetchScalarGridSpec(
        num_scalar_prefetch=1,
        grid=(batch, n_pages // group),
        in_specs=[per_b(q8)] + [paged(cache_kt, n) for n in range(group)]
                 + [paged(cache_vr, n) for n in range(group)]
                 + [per_b(k_new8), per_b(v_new8), per_b(g8), small(sub_w)]
                 + [small(a) for a in lams],
        out_specs=per_b(g8),
        scratch_shapes=[pltpu.VMEM((XROWS, 1), F32), pltpu.VMEM((XROWS, 1), F32),
                        pltpu.VMEM((XROWS, V_DIM), F32), pltpu.VMEM((XROWS, ATT_WIDTH), BF16),
                        pltpu.VMEM((page, ATT_WIDTH), F32), pltpu.VMEM((page, ATT_WIDTH), F32)],
    )
    return pl.pallas_call(
        functools.partial(_decode_attn_body, n_pages=n_pages, dec_seq=dec_seq, group=group),
        grid_spec=grid_spec,
        out_shape=jax.ShapeDtypeStruct(g8.shape, F32),
        compiler_params=_cparams(2),
        name="decode_attn",
    )(page_table, q8, *([cache_kt] * group), *([cache_vr] * group), k_new8, v_new8, g8, sub_w, *lams)


HEADS_PER_GROUP = SSM_HEADS // SSM_GROUPS
GROUP_WIDTH = SSM_WIDTH // SSM_GROUPS
TAIL = SUBLANES


def _pair_expand(a, j):
    lane = lax.broadcasted_iota(jnp.int32, (a.shape[0], LANES), 1)
    return jnp.where(lane < SSM_HEAD_DIM, a[:, 2 * j:2 * j + 1], a[:, 2 * j + 1:2 * j + 2])


def _ssd_body(xbc_ref, dt_ref, z_ref, cprev_ref, h0_ref, cw_ref, cb_ref, dtb_ref, alog_ref,
              dskip_ref, gn_ref,
              y_ref, cnew_ref, hnew_ref,
              xs_ref, dtp_ref, ht_ref, *, valid, n_chunks):
    q = CHUNK
    c = pl.program_id(1)
    first_step = (pl.program_id(0) == 0) & (c == 0)

    @pl.when(first_step)
    def _():
        xs_ref[...] = jnp.zeros(xs_ref.shape, F32)
        dtp_ref[...] = jnp.zeros(dtp_ref.shape, F32)

    @pl.when(c == 0)
    def _():
        xs_ref[TAIL - (CONV_W - 1):TAIL, :] = cprev_ref[...]
        ht_ref[...] = jnp.transpose(h0_ref[...])

    xs_ref[TAIL:TAIL + valid, :] = xbc_ref[...]
    dtp_ref[0:valid, :] = dt_ref[...]

    u = cb_ref[...]
    for w in range(CONV_W):
        u = u + xs_ref[pl.ds(TAIL - (CONV_W - 1) + w, q), :] * cw_ref[w:w + 1, :]
    u = _silu(u)

    @pl.when(c == n_chunks - 1)
    def _():
        cnew_ref[...] = xs_ref[pl.ds(TAIL + valid - (CONV_W - 1), CONV_W - 1), :]

    if n_chunks > 1:
        xs_ref[0:TAIL, :] = xs_ref[q:q + TAIL, :]

    xs = u[:, :SSM_WIDTH]
    bm = u[:, SSM_WIDTH:SSM_WIDTH + SSM_GROUPS * D_STATE]
    cm = u[:, SSM_WIDTH + SSM_GROUPS * D_STATE:]

    row = lax.broadcasted_iota(jnp.int32, (q, LANES), 0)
    x_dt = dtp_ref[...] + dtb_ref[...]
    dt = jnp.maximum(x_dt, 0.0) + jnp.log1p(jnp.exp(-jnp.abs(x_dt)))
    if valid < q:
        dt = jnp.where(row < valid, dt, 0.0)
    a = -jnp.exp(alog_ref[...])
    da = dt * a

    tt = lax.broadcasted_iota(jnp.int32, (q, q), 0)
    ss = lax.broadcasted_iota(jnp.int32, (q, q), 1)
    causal = ss <= tt
    tril = jnp.where(causal, 1.0, 0.0).astype(BF16)
    da_hi = da.astype(BF16)
    r1 = da - da_hi.astype(F32)
    da_mid = r1.astype(BF16)
    da_lo = (r1 - da_mid.astype(F32)).astype(BF16)
    cs3 = jnp.dot(tril, jnp.concatenate([da_hi, da_mid, da_lo], axis=1), preferred_element_type=F32)
    cs = cs3[:, :LANES] + cs3[:, LANES:2 * LANES] + cs3[:, 2 * LANES:]

    cs_t = jnp.transpose(cs)
    dt_t = jnp.transpose(dt)
    cs_last = cs[q - 1:q, :]
    e_cs = jnp.exp(cs)
    dec_end = jnp.exp(cs_last - cs) * dt
    chunk_decay = jnp.exp(cs_last)

    ht = ht_ref[...]
    ht_b = ht.astype(BF16)
    y_parts = []
    st_parts = []
    for grp in range(SSM_GROUPS):
        bg = bm[:, grp * D_STATE:(grp + 1) * D_STATE]
        cg = cm[:, grp * D_STATE:(grp + 1) * D_STATE].astype(BF16)
        cb = lax.dot_general(cg, bg.astype(BF16), (((1,), (1,)), ((), ())),
                             preferred_element_type=F32)
        y_off = jnp.dot(cg, ht_b[:, grp * GROUP_WIDTH:(grp + 1) * GROUP_WIDTH],
                        preferred_element_type=F32)
        xd_parts = []
        for jp in range(HEADS_PER_GROUP // 2):
            j = grp * (HEADS_PER_GROUP // 2) + jp
            x_pair = xs[:, j * LANES:(j + 1) * LANES]
            x_pair_b = x_pair.astype(BF16)
            y_pair = []
            for h in (2 * j, 2 * j + 1):
                diff = cs[:, h:h + 1] - cs_t[h:h + 1, :]
                lmat = jnp.exp(jnp.where(causal, diff, NEG_BIG))
                wmat = (cb * lmat * dt_t[h:h + 1, :]).astype(BF16)
                y_pair.append(jnp.dot(wmat, x_pair_b, preferred_element_type=F32))
            lane = lax.broadcasted_iota(jnp.int32, (q, LANES), 1)
            y_diag = jnp.where(lane < SSM_HEAD_DIM, y_pair[0], y_pair[1])
            y_parts.append(y_diag + y_off[:, jp * LANES:(jp + 1) * LANES] * _pair_expand(e_cs, j))
            xd_parts.append((x_pair * _pair_expand(dec_end, j)).astype(BF16))
        xd = jnp.concatenate(xd_parts, axis=1)
        st_parts.append(jnp.dot(jnp.transpose(bg).astype(BF16), xd, preferred_element_type=F32))
    decay_lanes = jnp.concatenate([_pair_expand(chunk_decay, j) for j in range(SSM_HEADS // 2)], axis=1)
    ht_new = ht * decay_lanes + jnp.concatenate(st_parts, axis=1)
    ht_ref[...] = ht_new

    @pl.when(c == n_chunks - 1)
    def _():
        hnew_ref[...] = jnp.transpose(ht_new)

    y = jnp.concatenate(y_parts, axis=1) + dskip_ref[...] * xs
    y = y[0:valid] * _silu(z_ref[...])
    outs = []
    for grp in range(SSM_GROUPS):
        yg = y[:, grp * GROUP_WIDTH:(grp + 1) * GROUP_WIDTH]
        outs.append(yg * lax.rsqrt(jnp.mean(yg * yg, axis=-1, keepdims=True) + EPS))
    y_ref[...] = (jnp.concatenate(outs, axis=1) * gn_ref[...]).astype(y_ref.dtype)


def _ssd(xbc, dt_raw, z, conv_prev, h0, w):
    n_seq, rows, _ = xbc.shape
    valid = min(rows, CHUNK)
    n_chunks = rows // valid
    assert n_chunks * valid == rows and valid >= CONV_W - 1
    blk = lambda n: pl.BlockSpec((None, valid, n), lambda s, c: (s, c, 0))
    per_seq = lambda a: pl.BlockSpec((None,) + a.shape[1:], lambda s, c: (s, 0, 0))
    small = lambda a: pl.BlockSpec(a.shape, lambda s, c: (0, 0))
    consts = [w["conv_w"], w["conv_b"], w["dt_bias"], w["a_log"], w["d_skip"], w["gnorm"]]
    return pl.pallas_call(
        functools.partial(_ssd_body, valid=valid, n_chunks=n_chunks),
        grid=(n_seq, n_chunks),
        in_specs=[blk(CONV_DIM), blk(LANES), blk(SSM_WIDTH), per_seq(conv_prev), per_seq(h0)]
                 + [small(a) for a in consts],
        out_specs=[blk(SSM_WIDTH), per_seq(conv_prev), per_seq(h0)],
        out_shape=(jax.ShapeDtypeStruct((n_seq, rows, SSM_WIDTH), BF16),
                   jax.ShapeDtypeStruct(conv_prev.shape, F32),
                   jax.ShapeDtypeStruct(h0.shape, F32)),
        scratch_shapes=[pltpu.VMEM((CHUNK + TAIL, CONV_DIM), F32),
                        pltpu.VMEM((CHUNK, LANES), F32),
                        pltpu.VMEM((D_STATE, SSM_WIDTH), F32)],
        compiler_params=_cparams(2),
        name="ssd",
    )(xbc, dt_raw, z, conv_prev, h0, *consts)


def _outproj_body(att_ref, ssm_ref, x_ref, p_ref, woa_ref, wos_ref, wpp_ref, wpg_ref, fn_ref, y_ref):
    hmid = (x_ref[...]
            + jnp.dot(att_ref[...], woa_ref[...], preferred_element_type=F32)
            + jnp.dot(ssm_ref[...], wos_ref[...], preferred_element_type=F32))
    emb = jnp.dot(p_ref[...].astype(BF16), wpp_ref[...], preferred_element_type=F32)
    gate = jax.nn.sigmoid(jnp.dot(hmid.astype(BF16), wpg_ref[...], preferred_element_type=F32))
    out = hmid + emb * gate
    y_ref[...] = out * lax.rsqrt(jnp.mean(out * out, axis=-1, keepdims=True) + EPS) * fn_ref[...]


def _outproj(att, ssm, x, p, w, tm):
    rows = x.shape[0]
    row = lambda n: pl.BlockSpec((tm, n), lambda i: (i, 0))
    wspec = lambda a: pl.BlockSpec(a.shape, lambda i: (0, 0))
    ws = [w["out_att"], w["out_ssm"], w["ple_proj"], w["ple_gate"], w["final_norm"]]
    return pl.pallas_call(
        _outproj_body,
        grid=(rows // tm,),
        in_specs=[row(ATT_WIDTH), row(SSM_WIDTH), row(D_MODEL), row(PLE_DIM)] + [wspec(a) for a in ws],
        out_specs=row(D_MODEL),
        out_shape=jax.ShapeDtypeStruct((rows, D_MODEL), F32),
        compiler_params=_cparams(1),
        name="outproj",
    )(att, ssm, x, p, *ws)


def _rope_tables(pos):
    half = QK_DIM // 2
    inv = ROPE_THETA ** (-jnp.arange(0, QK_DIM, 2, dtype=F32) / QK_DIM)
    ang = pos.astype(F32)[:, None] * inv[None, :]
    cos = jnp.tile(jnp.cos(ang), (1, LANES // half))
    sin = jnp.sin(ang)
    sin = jnp.tile(jnp.concatenate([-sin, sin], axis=1), (1, LANES // QK_DIM))
    return cos, sin


def _pad_to(a, axis, size):
    pads = [(0, 0)] * a.ndim
    pads[axis] = (0, size - a.shape[axis])
    return jnp.pad(a, pads)


def kernel(x_prompt, x_sample, cache_k, cache_v, state_conv, state_ssm, page_table, p_prompt, p_sample, w_norm, w_in, lambda_q1, lambda_k1, lambda_q2, lambda_k2, subln_w, conv_w, conv_b, dt_bias, A_log, D_skip, gnorm_w, w_out, w_ple_proj, w_ple_gate, final_norm_w):
    assert w_norm.shape[0] == 1 and x_prompt.shape[0] == 1
    seq = x_prompt.shape[1]
    dec_batch, dec_seq, _ = x_sample.shape
    n_pool, page = cache_k.shape[1], cache_k.shape[2]
    past = page_table.shape[1] * page

    w_in0 = w_in[0]
    splits = [0, 1024, 2048, 3072, 4096, 5120, 5120 + CONV_DIM, 5120 + CONV_DIM + SSM_HEADS]
    names = ["q", "k", "v", "g", "z", "xbc", "dt"]
    w = {n: w_in0[:, a:b].astype(BF16) for n, a, b in zip(names, splits[:-1], splits[1:])}
    w["dt"] = _pad_to(w["dt"], 1, LANES)
    w["norm"] = w_norm
    ssd_w = {"conv_w": conv_w[0], "conv_b": conv_b, "dt_bias": _pad_to(dt_bias, 1, LANES),
             "a_log": _pad_to(A_log, 1, LANES), "d_skip": jnp.repeat(D_skip, SSM_HEAD_DIM, axis=1),
             "gnorm": gnorm_w}
    out_w = {"out_att": w_out[0, :ATT_WIDTH].astype(BF16), "out_ssm": w_out[0, ATT_WIDTH:].astype(BF16),
             "ple_proj": w_ple_proj[0].astype(BF16), "ple_gate": w_ple_gate[0].astype(BF16),
             "final_norm": final_norm_w[None, :]}
    lams = [lambda_q1, lambda_k1, lambda_q2, lambda_k2]

    cos_p, sin_p = _rope_tables(jnp.arange(seq))
    q_p, k_p, v_p, g_p, z_p, xbc_p, dt_p, kb_p, vt_p = _inproj(
        x_prompt[0], w, cos_p, sin_p, tm=INPROJ_TM, attn_tk=ATTN_TK)
    att_p = _prompt_attn(q_p, kb_p, vt_p, g_p, subln_w, lams, tq=ATTN_TQ, tk=ATTN_TK)
    ssm_p, conv_p, h_p = _ssd(xbc_p[None], dt_p[None], z_p[None],
                              jnp.zeros((1, CONV_W - 1, CONV_DIM), F32),
                              jnp.zeros((1, SSM_WIDTH, D_STATE), F32), ssd_w)
    y_p = _outproj(att_p, ssm_p[0], x_prompt[0], p_prompt[0, 0], out_w, tm=OUTPROJ_TM)

    n_rows = dec_batch * dec_seq
    cos_s, sin_s = _rope_tables(past + jnp.arange(n_rows) % dec_seq)
    q_s, k_s, v_s, g_s, z_s, xbc_s, dt_s = _inproj(x_sample.reshape(n_rows, D_MODEL), w,
                                                 cos_s, sin_s, tm=n_rows)
    per_seq = lambda a: a.reshape(dec_batch, dec_seq, a.shape[-1])
    q3 = per_seq(q_s)
    cache_kt = jnp.transpose(cache_k[0], (0, 2, 3, 4, 1)).reshape(n_pool, ATT_WIDTH, page)
    cache_vr = cache_v[0].reshape(n_pool, page * ATT_HEADS, V_DIM)
    pad8 = lambda a: _pad_to(per_seq(a), 1, SUBLANES)
    att_s = _decode_attn(page_table, jnp.concatenate([q3, q3], axis=1).astype(F32), cache_kt, cache_vr,
                         pad8(k_s), pad8(v_s), pad8(g_s), subln_w, lams, dec_seq, DECODE_PAGES)
    att_s = att_s[:, :dec_seq].reshape(n_rows, ATT_WIDTH).astype(BF16)
    ssm_s, conv_s, h_s = _ssd(per_seq(xbc_s), per_seq(dt_s), per_seq(z_s), state_conv[0],
                              state_ssm[0].reshape(dec_batch, SSM_WIDTH, D_STATE), ssd_w)
    y_s = _outproj(att_s, ssm_s.reshape(n_rows, SSM_WIDTH), x_sample.reshape(n_rows, D_MODEL),
                   p_sample[0].reshape(n_rows, PLE_DIM), out_w, tm=n_rows)

    hp = (SSM_HEADS, SSM_HEAD_DIM, D_STATE)
    return (y_p[None],
            y_s.reshape(dec_batch, dec_seq, D_MODEL),
            k_p.reshape(1, 1, seq, ATT_HEADS, 2, QK_DIM),
            v_p.reshape(1, 1, seq, ATT_HEADS, V_DIM),
            conv_p[None],
            h_p.reshape((1, 1) + hp),
            k_s.reshape(1, dec_batch, dec_seq, ATT_HEADS, 2, QK_DIM),
            v_s.reshape(1, dec_batch, dec_seq, ATT_HEADS, V_DIM),
            conv_s[None],
            h_s.reshape((1, dec_batch) + hp))
```

```python
import functools
import math

import jax
import jax.numpy as jnp
from jax import lax
from jax.experimental import pallas as pl
from jax.experimental.pallas import tpu as pltpu

F32 = jnp.float32
BF16 = jnp.bfloat16

D_MODEL = 1024
ATT_HEADS = 8
QK_DIM = 64
V_DIM = 2 * QK_DIM
ATT_WIDTH = ATT_HEADS * V_DIM
SM_SCALE = QK_DIM ** -0.5
LOG2E = math.log2(math.e)
ROPE_THETA = 10000.0
SSM_WIDTH = 1024
SSM_HEAD_DIM = 64
SSM_HEADS = SSM_WIDTH // SSM_HEAD_DIM
SSM_GROUPS = 2
D_STATE = 128
CONV_W = 4
CONV_DIM = SSM_WIDTH + 2 * SSM_GROUPS * D_STATE
CHUNK = 128
PLE_DIM = 256
EPS = 1e-6
LAM_INIT = 0.8 - 0.6 * math.exp(-0.3 * 0)

LANES = 128
SUBLANES = 8
BF16_ROWS = 16
NEG_BIG = -1e30
VMEM_LIMIT = 56 * 1024 * 1024

INPROJ_TM = 256
ATTN_TQ = 512
ATTN_TK = 512
UNROLL = 2
HEADS_PER_STEP = 2
DECODE_PAGES = 16


def _cparams(n_axes):
    return pltpu.CompilerParams(dimension_semantics=("arbitrary",) * n_axes,
                                vmem_limit_bytes=VMEM_LIMIT)


def _silu(x):
    return x * jax.nn.sigmoid(x)


def _inproj_body(x_ref, wn_ref, wq_ref, wk_ref, wv_ref, wg_ref, wz_ref, wx_ref, wdt_ref,
                 cos_ref, sin_ref, q_ref, k_ref, v_ref, g_ref, z_ref, xbc_ref, dt_ref,
                 *attn_copies):
    x = x_ref[...]
    hn = x * lax.rsqrt(jnp.mean(x * x, axis=-1, keepdims=True) + EPS) * wn_ref[...]
    hn = hn.astype(BF16)
    cos = cos_ref[...]
    sin = sin_ref[...]
    lane = lax.broadcasted_iota(jnp.int32, cos.shape, 1)
    first_half = (lane % QK_DIM) < (QK_DIM // 2)

    def rope_chunks(t):
        for c in range(t.shape[1] // LANES):
            tc = t[:, c * LANES:(c + 1) * LANES]
            partner = jnp.where(first_half,
                                pltpu.roll(tc, LANES - QK_DIM // 2, 1),
                                pltpu.roll(tc, QK_DIM // 2, 1))
            yield c, tc * cos + partner * sin

    q = jnp.dot(hn, wq_ref[...], preferred_element_type=F32)
    for c, r in rope_chunks(q):
        r = r * (SM_SCALE * LOG2E)
        if attn_copies:
            q_ref[c] = r.astype(BF16)
        else:
            q_ref[:, c * LANES:(c + 1) * LANES] = r.astype(BF16)
    k = jnp.dot(hn, wk_ref[...], preferred_element_type=F32)
    for c, r in rope_chunks(k):
        k_ref[:, c * LANES:(c + 1) * LANES] = r
        if attn_copies:
            attn_copies[0][c] = r.astype(BF16)
    v = jnp.dot(hn, wv_ref[...], preferred_element_type=F32)
    v_ref[...] = v
    if attn_copies:
        attn_copies[1][...] = jnp.transpose(v).astype(BF16)
    g_ref[...] = jnp.dot(hn, wg_ref[...], preferred_element_type=F32)
    z_ref[...] = jnp.dot(hn, wz_ref[...], preferred_element_type=F32)
    xbc_ref[...] = jnp.dot(hn, wx_ref[...], preferred_element_type=F32)
    dt_ref[...] = jnp.dot(hn, wdt_ref[...], preferred_element_type=F32)


def _inproj(x, w, cos, sin, tm, attn_tk=None):
    rows = x.shape[0]
    row = lambda n: pl.BlockSpec((tm, n), lambda i: (i, 0))
    wspec = lambda a: pl.BlockSpec(a.shape, lambda i: (0, 0), pipeline_mode=pl.Buffered(1))
    q_shape = (ATT_HEADS, rows, V_DIM) if attn_tk is not None else (rows, ATT_WIDTH)
    out_shapes = [
        jax.ShapeDtypeStruct(q_shape, BF16),
        jax.ShapeDtypeStruct((rows, ATT_WIDTH), F32),
        jax.ShapeDtypeStruct((rows, ATT_WIDTH), F32),
        jax.ShapeDtypeStruct((rows, ATT_WIDTH), F32),
        jax.ShapeDtypeStruct((rows, SSM_WIDTH), F32),
        jax.ShapeDtypeStruct((rows, CONV_DIM), F32),
        jax.ShapeDtypeStruct((rows, LANES), F32),
    ]
    out_specs = [row(s.shape[1]) for s in out_shapes]
    if attn_tk is not None:
        per_tile = attn_tk // tm
        assert per_tile * tm == attn_tk and rows % attn_tk == 0
        out_shapes += [jax.ShapeDtypeStruct((ATT_HEADS, rows, V_DIM), BF16),
                       jax.ShapeDtypeStruct((rows // attn_tk, ATT_WIDTH, attn_tk), BF16)]
        out_specs[0] = pl.BlockSpec((ATT_HEADS, tm, V_DIM), lambda i: (0, i, 0))
        out_specs += [pl.BlockSpec((ATT_HEADS, tm, V_DIM), lambda i: (0, i, 0)),
                      pl.BlockSpec((None, ATT_WIDTH, tm), lambda i: (i // per_tile, 0, i % per_tile))]
    return pl.pallas_call(
        _inproj_body,
        grid=(rows // tm,),
        in_specs=[row(D_MODEL), wspec(w["norm"]), wspec(w["q"]), wspec(w["k"]), wspec(w["v"]),
                  wspec(w["g"]), wspec(w["z"]), wspec(w["xbc"]), wspec(w["dt"]),
                  row(LANES), row(LANES)],
        out_specs=out_specs,
        out_shape=out_shapes,
        compiler_params=_cparams(1),
        name="inproj",
    )(x, w["norm"], w["q"], w["k"], w["v"], w["g"], w["z"], w["xbc"], w["dt"], cos, sin)


def _lambda_value(lq1_ref, lk1_ref, lq2_ref, lk2_ref):
    a = jnp.sum(lq1_ref[...] * lk1_ref[...], axis=-1, keepdims=True)
    b = jnp.sum(lq2_ref[...] * lk2_ref[...], axis=-1, keepdims=True)
    return jnp.exp(a) - jnp.exp(b) + LAM_INIT


def _sub_norm_gate(o, sub_w, g):
    o = o * lax.rsqrt(jnp.mean(o * o, axis=-1, keepdims=True) + EPS) * sub_w * (1.0 - LAM_INIT)
    return o * _silu(g)


def _prompt_attn_body(q_ref, k_ref, vt_ref, g_ref, sub_ref, lq1_ref, lk1_ref, lq2_ref, lk2_ref,
                      o_ref, m_ref, l_ref, acc_ref, s_ref, mx_ref, bias_ref, *, tq, tk):
    i = pl.program_id(1)

    @pl.when((pl.program_id(0) == 0) & (i == 0))
    def _():
        r = lax.broadcasted_iota(jnp.int32, bias_ref.shape, 0)
        c = lax.broadcasted_iota(jnp.int32, bias_ref.shape, 1) % tq
        bias_ref[...] = jnp.where(r <= c, 0.0, NEG_BIG)

    m_ref[...] = jnp.full(m_ref.shape, NEG_BIG, F32)
    l_ref[...] = jnp.zeros(l_ref.shape, F32)
    acc_ref[...] = jnp.zeros(acc_ref.shape, F32)

    zero = jnp.zeros((QK_DIM, tq), BF16)
    q_blk = []
    for hh in range(HEADS_PER_STEP):
        qt = jnp.transpose(q_ref[hh].astype(F32)).astype(BF16)
        q_blk.append(jnp.concatenate([jnp.concatenate([qt[:QK_DIM], zero], axis=0),
                                      jnp.concatenate([zero, qt[QK_DIM:]], axis=0)], axis=1))

    def scores(hh, t):
        kt = k_ref[hh, pl.ds(pl.multiple_of(t * tk, tk), tk), :]
        s = jnp.dot(kt, q_blk[hh], preferred_element_type=F32)
        s_ref[hh] = s
        mx_ref[hh] = jnp.max(s, axis=0, keepdims=True)

    def update(hh, t, diagonal):
        s = s_ref[hh]
        if diagonal:
            s = s + bias_ref[...]
            mx = jnp.max(s, axis=0, keepdims=True)
        else:
            mx = mx_ref[hh]
        m_prev = m_ref[hh]
        m_new = jnp.maximum(m_prev, mx)
        alpha = jnp.exp2(m_prev - m_new)
        p = jnp.exp2(s - m_new).astype(BF16)
        v_ext = jnp.concatenate([vt_ref[t, hh * V_DIM:(hh + 1) * V_DIM, :], jnp.ones((BF16_ROWS, tk), BF16)],
                                axis=0)
        pv = jnp.dot(v_ext, p, preferred_element_type=F32)
        l_ref[hh] = alpha * l_ref[hh] + pv[V_DIM:V_DIM + 1]
        acc_ref[hh] = alpha * acc_ref[hh] + pv[:V_DIM]
        m_ref[hh] = m_new

    def run(t0, count):
        for k in range(count):
            scores(1, t0 + k)
            update(0, t0 + k, False)
            scores(0, t0 + k + 1)
            update(1, t0 + k, False)

    n_full = (i * tq) // tk
    scores(0, 0)

    def unrolled(j, carry):
        run(UNROLL * j, UNROLL)
        return carry

    lax.fori_loop(0, n_full // UNROLL, unrolled, 0)
    t_rem = (n_full // UNROLL) * UNROLL
    for rem in range(UNROLL):
        @pl.when(n_full % UNROLL == rem)
        def _(rem=rem):
            run(t_rem, rem)
            scores(1, n_full)
            update(0, n_full, True)
            update(1, n_full, True)

    lam = _lambda_value(lq1_ref, lk1_ref, lq2_ref, lk2_ref)
    for hh in range(HEADS_PER_STEP):
        cols = slice(hh * V_DIM, (hh + 1) * V_DIM)
        o_both = acc_ref[hh] * (1.0 / l_ref[hh])
        o = jnp.transpose(o_both[:, :tq] - lam * o_both[:, tq:])
        o_ref[:, cols] = _sub_norm_gate(o, sub_ref[...], g_ref[:, cols]).astype(o_ref.dtype)


def _prompt_attn(q, k, vt, g, sub_w, lams, tq, tk):
    seq = q.shape[1]
    hps = HEADS_PER_STEP
    assert tq == tk and seq % tq == 0 and vt.shape == (seq // tk, ATT_WIDTH, tk) and ATT_HEADS % hps == 0
    row_tile = pl.BlockSpec((tq, hps * V_DIM), lambda h, i: (i, h))
    q_tile = pl.BlockSpec((hps, tq, V_DIM), lambda h, i: (h, i, 0))
    k_all = pl.BlockSpec((hps, seq, V_DIM), lambda h, i: (h, 0, 0))
    vt_all = pl.BlockSpec((seq // tk, hps * V_DIM, tk), lambda h, i: (0, h, 0))
    small = lambda a: pl.BlockSpec(a.shape, lambda h, i: (0, 0))
    return pl.pallas_call(
        functools.partial(_prompt_attn_body, tq=tq, tk=tk),
        grid=(ATT_HEADS // hps, seq // tq),
        in_specs=[q_tile, k_all, vt_all, row_tile, small(sub_w)] + [small(a) for a in lams],
        out_specs=row_tile,
        out_shape=jax.ShapeDtypeStruct((seq, ATT_WIDTH), BF16),
        scratch_shapes=[pltpu.VMEM((hps, 1, 2 * tq), F32), pltpu.VMEM((hps, 1, 2 * tq), F32),
                        pltpu.VMEM((hps, V_DIM, 2 * tq), F32),
                        pltpu.VMEM((hps, tk, 2 * tq), F32), pltpu.VMEM((hps, 1, 2 * tq), F32),
                        pltpu.VMEM((tk, 2 * tq), F32)],
        compiler_params=_cparams(2),
        name="prompt_attn",
    )(q, k, vt, g, sub_w, *lams)


XROWS = ATT_HEADS * SUBLANES


N_OUTPROJ_IN = 9
OUTPROJ_EVERY = 2


def _decode_attn_body(pt_ref, *refs, n_pages, dec_seq, group):
    del pt_ref
    q_ref = refs[0]
    k_refs = refs[1:1 + group]
    v_refs = refs[1 + group:1 + 2 * group]
    rest = refs[1 + 2 * group:]
    (kn_ref, vn_ref, g_ref, sub_ref, lq1_ref, lk1_ref, lq2_ref, lk2_ref) = rest[:8]
    outproj_in = rest[8:8 + N_OUTPROJ_IN]
    (o_ref, y_ref, m_ref, l_ref, acc_ref, qblk_ref, kbuf_ref, vbuf_ref) = rest[8 + N_OUTPROJ_IN:]
    step = pl.program_id(1)
    page = LANES

    @pl.when(step == 0)
    def _():
        m_ref[...] = jnp.full(m_ref.shape, NEG_BIG, F32)
        l_ref[...] = jnp.zeros(l_ref.shape, F32)
        acc_ref[...] = jnp.zeros(acc_ref.shape, F32)
        q8 = q_ref[...]
        row = lax.broadcasted_iota(jnp.int32, q8.shape, 0)
        lane = lax.broadcasted_iota(jnp.int32, q8.shape, 1)
        own_map = ((lane % V_DIM) < QK_DIM) == (row < dec_seq)
        for h in range(ATT_HEADS):
            keep = own_map & ((lane // V_DIM) == h)
            qblk_ref[h * SUBLANES:(h + 1) * SUBLANES, :] = jnp.where(keep, q8, 0.0).astype(BF16)

    qblk = qblk_ref[...]
    first_of_pair = lax.broadcasted_iota(jnp.int32, (BF16_ROWS, V_DIM), 0) < SUBLANES

    def process(s, value_fn):
        m_prev = m_ref[...]
        m_new = jnp.maximum(m_prev, jnp.max(s, axis=-1, keepdims=True))
        alpha = jnp.exp2(m_prev - m_new)
        p = jnp.exp2(s - m_new)
        l_ref[...] = alpha * l_ref[...] + jnp.sum(p, axis=-1, keepdims=True)
        p = p.astype(BF16)
        pv_parts = []
        for pair in range(ATT_HEADS // 2):
            p_pair = p[pair * BF16_ROWS:(pair + 1) * BF16_ROWS]
            halves = [jnp.dot(p_pair, value_fn(h).astype(BF16), preferred_element_type=F32)
                      for h in (2 * pair, 2 * pair + 1)]
            pv_parts.append(jnp.where(first_of_pair, halves[0], halves[1]))
        acc_ref[...] = alpha * acc_ref[...] + jnp.concatenate(pv_parts, axis=0)
        m_ref[...] = m_new

    s_pages = jnp.concatenate([jnp.dot(qblk, kp[...].astype(BF16), preferred_element_type=F32)
                               for kp in k_refs], axis=1)
    process(s_pages, lambda h: jnp.concatenate(
        [vp[pl.ds(h, page, stride=ATT_HEADS), :] for vp in v_refs], axis=0))

    @pl.when(step == n_pages // group - 1)
    def _():
        kbuf_ref[...] = jnp.zeros(kbuf_ref.shape, F32)
        vbuf_ref[...] = jnp.zeros(vbuf_ref.shape, F32)
        kbuf_ref[0:SUBLANES, :] = kn_ref[...]
        vbuf_ref[0:SUBLANES, :] = vn_ref[...]
        s_new = lax.dot_general(qblk, kbuf_ref[...].astype(BF16), (((1,), (1,)), ((), ())),
                                preferred_element_type=F32)
        qi = lax.broadcasted_iota(jnp.int32, s_new.shape, 0) % dec_seq
        kj = lax.broadcasted_iota(jnp.int32, s_new.shape, 1)
        process(jnp.where(kj <= qi, s_new, NEG_BIG), lambda h: vbuf_ref[:, h * V_DIM:(h + 1) * V_DIM])

        t = acc_ref[...] / l_ref[...]
        lam = _lambda_value(lq1_ref, lk1_ref, lq2_ref, lk2_ref)
        g = g_ref[...]
        for h in range(ATT_HEADS):
            cols = slice(h * V_DIM, (h + 1) * V_DIM)
            th = t[h * SUBLANES:(h + 1) * SUBLANES]
            o = th - lam * pltpu.roll(th, SUBLANES - dec_seq, 0)
            o_ref[:, cols] = _sub_norm_gate(o, sub_ref[...], g[:, cols])

    @pl.when((pl.program_id(0) * (n_pages // group) + step) % OUTPROJ_EVERY == 0)
    def _():
        _outproj_body(*outproj_in, y_ref)


def _decode_attn(page_table, q8, cache_kt, cache_vr, k_new8, v_new8, g8, sub_w, lams, dec_seq, group,
                 outproj_rows, outproj_w):
    batch, n_pages = page_table.shape
    page = cache_kt.shape[2]
    assert page == LANES and 2 * dec_seq == SUBLANES and n_pages % group == 0
    steps = n_pages // group
    rows = outproj_rows[2].shape[0]
    n_tiles = batch * steps // OUTPROJ_EVERY
    tm = rows // n_tiles
    assert tm * n_tiles == rows and n_tiles * OUTPROJ_EVERY == batch * steps and tm % BF16_ROWS == 0
    row_tile = lambda a: pl.BlockSpec((tm, a.shape[1]), lambda b, p, pt: ((b * steps + p) // OUTPROJ_EVERY, 0))
    weight = lambda a: pl.BlockSpec(a.shape, lambda b, p, pt: (0, 0), pipeline_mode=pl.Buffered(1))
    outproj_ws = _outproj_weights(outproj_w)
    per_b = lambda a: pl.BlockSpec((None,) + a.shape[1:], lambda b, p, pt: (b, 0, 0))
    paged = lambda a, n: pl.BlockSpec((None,) + a.shape[1:],
                                      lambda b, p, pt: (pt[b, p * group + n], 0, 0))
    small = lambda a: pl.BlockSpec(a.shape, lambda b, p, pt: (0, 0))
    grid_spec = pltpu.PrefetchScalarGridSpec(
        num_scalar_prefetch=1,
        grid=(batch, n_pages // group),
        in_specs=[per_b(q8)] + [paged(cache_kt, n) for n in range(group)]
                 + [paged(cache_vr, n) for n in range(group)]
                 + [per_b(k_new8), per_b(v_new8), per_b(g8), small(sub_w)]
                 + [small(a) for a in lams]
                 + [row_tile(a) for a in outproj_rows] + [weight(a) for a in outproj_ws],
        out_specs=[per_b(g8), row_tile(outproj_rows[2])],
        scratch_shapes=[pltpu.VMEM((XROWS, 1), F32), pltpu.VMEM((XROWS, 1), F32),
                        pltpu.VMEM((XROWS, V_DIM), F32), pltpu.VMEM((XROWS, ATT_WIDTH), BF16),
                        pltpu.VMEM((page, ATT_WIDTH), F32), pltpu.VMEM((page, ATT_WIDTH), F32)],
    )
    return pl.pallas_call(
        functools.partial(_decode_attn_body, n_pages=n_pages, dec_seq=dec_seq, group=group),
        grid_spec=grid_spec,
        out_shape=(jax.ShapeDtypeStruct(g8.shape, F32), jax.ShapeDtypeStruct(outproj_rows[2].shape, F32)),
        compiler_params=_cparams(2),
        name="decode_attn_outproj",
    )(page_table, q8, *([cache_kt] * group), *([cache_vr] * group), k_new8, v_new8, g8, sub_w, *lams,
      *outproj_rows, *outproj_ws)


HEADS_PER_GROUP = SSM_HEADS // SSM_GROUPS
GROUP_WIDTH = SSM_WIDTH // SSM_GROUPS
TAIL = SUBLANES


def _pair_expand(a, j):
    lane = lax.broadcasted_iota(jnp.int32, (a.shape[0], LANES), 1)
    return jnp.where(lane < SSM_HEAD_DIM, a[:, 2 * j:2 * j + 1], a[:, 2 * j + 1:2 * j + 2])


def _ssd_body(xbc_ref, dt_ref, z_ref, cprev_ref, h0_ref, cw_ref, cb_ref, dtb_ref, alog_ref,
              dskip_ref, gn_ref,
              y_ref, cnew_ref, hnew_ref,
              xs_ref, dtp_ref, ht_ref, *, valid, n_chunks):
    q = CHUNK
    c = pl.program_id(1)
    first_step = (pl.program_id(0) == 0) & (c == 0)

    @pl.when(first_step)
    def _():
        xs_ref[...] = jnp.zeros(xs_ref.shape, F32)
        dtp_ref[...] = jnp.zeros(dtp_ref.shape, F32)

    @pl.when(c == 0)
    def _():
        xs_ref[TAIL - (CONV_W - 1):TAIL, :] = cprev_ref[...]
        ht_ref[...] = jnp.transpose(h0_ref[...])

    xs_ref[TAIL:TAIL + valid, :] = xbc_ref[...]
    dtp_ref[0:valid, :] = dt_ref[...]

    u = cb_ref[...]
    for w in range(CONV_W):
        u = u + xs_ref[pl.ds(TAIL - (CONV_W - 1) + w, q), :] * cw_ref[w:w + 1, :]
    u = _silu(u)

    @pl.when(c == n_chunks - 1)
    def _():
        cnew_ref[...] = xs_ref[pl.ds(TAIL + valid - (CONV_W - 1), CONV_W - 1), :]

    if n_chunks > 1:
        xs_ref[0:TAIL, :] = xs_ref[q:q + TAIL, :]

    xs = u[:, :SSM_WIDTH]
    bm = u[:, SSM_WIDTH:SSM_WIDTH + SSM_GROUPS * D_STATE]
    cm = u[:, SSM_WIDTH + SSM_GROUPS * D_STATE:]

    row = lax.broadcasted_iota(jnp.int32, (q, LANES), 0)
    x_dt = dtp_ref[...] + dtb_ref[...]
    dt = jnp.maximum(x_dt, 0.0) + jnp.log1p(jnp.exp(-jnp.abs(x_dt)))
    if valid < q:
        dt = jnp.where(row < valid, dt, 0.0)
    a = -jnp.exp(alog_ref[...])
    da = dt * a

    tt = lax.broadcasted_iota(jnp.int32, (q, q), 0)
    ss = lax.broadcasted_iota(jnp.int32, (q, q), 1)
    causal = ss <= tt
    tril = jnp.where(causal, 1.0, 0.0).astype(BF16)
    da_hi = da.astype(BF16)
    r1 = da - da_hi.astype(F32)
    da_mid = r1.astype(BF16)
    da_lo = (r1 - da_mid.astype(F32)).astype(BF16)
    cs3 = jnp.dot(tril, jnp.concatenate([da_hi, da_mid, da_lo], axis=1), preferred_element_type=F32)
    cs = cs3[:, :LANES] + cs3[:, LANES:2 * LANES] + cs3[:, 2 * LANES:]

    cs_t = jnp.transpose(cs)
    dt_t = jnp.transpose(dt)
    cs_last = cs[q - 1:q, :]
    e_cs = jnp.exp(cs)
    dec_end = jnp.exp(cs_last - cs) * dt
    chunk_decay = jnp.exp(cs_last)

    ht = ht_ref[...]
    ht_b = ht.astype(BF16)
    y_parts = []
    st_parts = []
    for grp in range(SSM_GROUPS):
        bg = bm[:, grp * D_STATE:(grp + 1) * D_STATE]
        cg = cm[:, grp * D_STATE:(grp + 1) * D_STATE].astype(BF16)
        cb = lax.dot_general(cg, bg.astype(BF16), (((1,), (1,)), ((), ())),
                             preferred_element_type=F32)
        y_off = jnp.dot(cg, ht_b[:, grp * GROUP_WIDTH:(grp + 1) * GROUP_WIDTH],
                        preferred_element_type=F32)
        xd_parts = []
        for jp in range(HEADS_PER_GROUP // 2):
            j = grp * (HEADS_PER_GROUP // 2) + jp
            x_pair = xs[:, j * LANES:(j + 1) * LANES]
            x_pair_b = x_pair.astype(BF16)
            y_pair = []
            for h in (2 * j, 2 * j + 1):
                diff = cs[:, h:h + 1] - cs_t[h:h + 1, :]
                lmat = jnp.exp(jnp.where(causal, diff, NEG_BIG))
                wmat = (cb * lmat * dt_t[h:h + 1, :]).astype(BF16)
                y_pair.append(jnp.dot(wmat, x_pair_b, preferred_element_type=F32))
            lane = lax.broadcasted_iota(jnp.int32, (q, LANES), 1)
            y_diag = jnp.where(lane < SSM_HEAD_DIM, y_pair[0], y_pair[1])
            y_parts.append(y_diag + y_off[:, jp * LANES:(jp + 1) * LANES] * _pair_expand(e_cs, j))
            xd_parts.append((x_pair * _pair_expand(dec_end, j)).astype(BF16))
        xd = jnp.concatenate(xd_parts, axis=1)
        st_parts.append(jnp.dot(jnp.transpose(bg).astype(BF16), xd, preferred_element_type=F32))
    decay_lanes = jnp.concatenate([_pair_expand(chunk_decay, j) for j in range(SSM_HEADS // 2)], axis=1)
    ht_new = ht * decay_lanes + jnp.concatenate(st_parts, axis=1)
    ht_ref[...] = ht_new

    @pl.when(c == n_chunks - 1)
    def _():
        hnew_ref[...] = jnp.transpose(ht_new)

    y = jnp.concatenate(y_parts, axis=1) + dskip_ref[...] * xs
    y = y[0:valid] * _silu(z_ref[...])
    outs = []
    for grp in range(SSM_GROUPS):
        yg = y[:, grp * GROUP_WIDTH:(grp + 1) * GROUP_WIDTH]
        outs.append(yg * lax.rsqrt(jnp.mean(yg * yg, axis=-1, keepdims=True) + EPS))
    y_ref[...] = (jnp.concatenate(outs, axis=1) * gn_ref[...]).astype(y_ref.dtype)


def _ssd(xbc, dt_raw, z, conv_prev, h0, w):
    n_seq, rows, _ = xbc.shape
    valid = min(rows, CHUNK)
    n_chunks = rows // valid
    assert n_chunks * valid == rows and valid >= CONV_W - 1
    blk = lambda n: pl.BlockSpec((None, valid, n), lambda s, c: (s, c, 0))
    per_seq = lambda a: pl.BlockSpec((None,) + a.shape[1:], lambda s, c: (s, 0, 0))
    small = lambda a: pl.BlockSpec(a.shape, lambda s, c: (0, 0))
    consts = [w["conv_w"], w["conv_b"], w["dt_bias"], w["a_log"], w["d_skip"], w["gnorm"]]
    return pl.pallas_call(
        functools.partial(_ssd_body, valid=valid, n_chunks=n_chunks),
        grid=(n_seq, n_chunks),
        in_specs=[blk(CONV_DIM), blk(LANES), blk(SSM_WIDTH), per_seq(conv_prev), per_seq(h0)]
                 + [small(a) for a in consts],
        out_specs=[blk(SSM_WIDTH), per_seq(conv_prev), per_seq(h0)],
        out_shape=(jax.ShapeDtypeStruct((n_seq, rows, SSM_WIDTH), BF16),
                   jax.ShapeDtypeStruct(conv_prev.shape, F32),
                   jax.ShapeDtypeStruct(h0.shape, F32)),
        scratch_shapes=[pltpu.VMEM((CHUNK + TAIL, CONV_DIM), F32),
                        pltpu.VMEM((CHUNK, LANES), F32),
                        pltpu.VMEM((D_STATE, SSM_WIDTH), F32)],
        compiler_params=_cparams(2),
        name="ssd",
    )(xbc, dt_raw, z, conv_prev, h0, *consts)


def _outproj_body(att_ref, ssm_ref, x_ref, p_ref, woa_ref, wos_ref, wpp_ref, wpg_ref, fn_ref, y_ref):
    hmid = (x_ref[...]
            + jnp.dot(att_ref[...], woa_ref[...], preferred_element_type=F32)
            + jnp.dot(ssm_ref[...], wos_ref[...], preferred_element_type=F32))
    emb = jnp.dot(p_ref[...].astype(BF16), wpp_ref[...], preferred_element_type=F32)
    gate = jax.nn.sigmoid(jnp.dot(hmid.astype(BF16), wpg_ref[...], preferred_element_type=F32))
    out = hmid + emb * gate
    y_ref[...] = out * lax.rsqrt(jnp.mean(out * out, axis=-1, keepdims=True) + EPS) * fn_ref[...]


def _outproj_weights(w):
    return [w["out_att"], w["out_ssm"], w["ple_proj"], w["ple_gate"], w["final_norm"]]


def _outproj(att, ssm, x, p, w, tm):
    rows = x.shape[0]
    row = lambda n: pl.BlockSpec((tm, n), lambda i: (i, 0))
    wspec = lambda a: pl.BlockSpec(a.shape, lambda i: (0, 0))
    ws = _outproj_weights(w)
    return pl.pallas_call(
        _outproj_body,
        grid=(rows // tm,),
        in_specs=[row(ATT_WIDTH), row(SSM_WIDTH), row(D_MODEL), row(PLE_DIM)] + [wspec(a) for a in ws],
        out_specs=row(D_MODEL),
        out_shape=jax.ShapeDtypeStruct((rows, D_MODEL), F32),
        compiler_params=_cparams(1),
        name="outproj",
    )(att, ssm, x, p, *ws)


def _rope_tables(pos):
    half = QK_DIM // 2
    inv = ROPE_THETA ** (-jnp.arange(0, QK_DIM, 2, dtype=F32) / QK_DIM)
    ang = pos.astype(F32)[:, None] * inv[None, :]
    cos = jnp.tile(jnp.cos(ang), (1, LANES // half))
    sin = jnp.sin(ang)
    sin = jnp.tile(jnp.concatenate([-sin, sin], axis=1), (1, LANES // QK_DIM))
    return cos, sin


def _pad_to(a, axis, size):
    pads = [(0, 0)] * a.ndim
    pads[axis] = (0, size - a.shape[axis])
    return jnp.pad(a, pads)


def kernel(x_prompt, x_sample, cache_k, cache_v, state_conv, state_ssm, page_table, p_prompt, p_sample, w_norm, w_in, lambda_q1, lambda_k1, lambda_q2, lambda_k2, subln_w, conv_w, conv_b, dt_bias, A_log, D_skip, gnorm_w, w_out, w_ple_proj, w_ple_gate, final_norm_w):
    assert w_norm.shape[0] == 1 and x_prompt.shape[0] == 1
    seq = x_prompt.shape[1]
    dec_batch, dec_seq, _ = x_sample.shape
    n_pool, page = cache_k.shape[1], cache_k.shape[2]
    past = page_table.shape[1] * page

    w_in0 = w_in[0]
    splits = [0, 1024, 2048, 3072, 4096, 5120, 5120 + CONV_DIM, 5120 + CONV_DIM + SSM_HEADS]
    names = ["q", "k", "v", "g", "z", "xbc", "dt"]
    w = {n: w_in0[:, a:b].astype(BF16) for n, a, b in zip(names, splits[:-1], splits[1:])}
    w["dt"] = _pad_to(w["dt"], 1, LANES)
    w["norm"] = w_norm
    ssd_w = {"conv_w": conv_w[0], "conv_b": conv_b, "dt_bias": _pad_to(dt_bias, 1, LANES),
             "a_log": _pad_to(A_log, 1, LANES), "d_skip": jnp.repeat(D_skip, SSM_HEAD_DIM, axis=1),
             "gnorm": gnorm_w}
    out_w = {"out_att": w_out[0, :ATT_WIDTH].astype(BF16), "out_ssm": w_out[0, ATT_WIDTH:].astype(BF16),
             "ple_proj": w_ple_proj[0].astype(BF16), "ple_gate": w_ple_gate[0].astype(BF16),
             "final_norm": final_norm_w[None, :]}
    lams = [lambda_q1, lambda_k1, lambda_q2, lambda_k2]

    cos_p, sin_p = _rope_tables(jnp.arange(seq))
    q_p, k_p, v_p, g_p, z_p, xbc_p, dt_p, kb_p, vt_p = _inproj(
        x_prompt[0], w, cos_p, sin_p, tm=INPROJ_TM, attn_tk=ATTN_TK)
    att_p = _prompt_attn(q_p, kb_p, vt_p, g_p, subln_w, lams, tq=ATTN_TQ, tk=ATTN_TK)
    ssm_p, conv_p, h_p = _ssd(xbc_p[None], dt_p[None], z_p[None],
                              jnp.zeros((1, CONV_W - 1, CONV_DIM), F32),
                              jnp.zeros((1, SSM_WIDTH, D_STATE), F32), ssd_w)

    n_rows = dec_batch * dec_seq
    cos_s, sin_s = _rope_tables(past + jnp.arange(n_rows) % dec_seq)
    q_s, k_s, v_s, g_s, z_s, xbc_s, dt_s = _inproj(x_sample.reshape(n_rows, D_MODEL), w,
                                                 cos_s, sin_s, tm=n_rows)
    per_seq = lambda a: a.reshape(dec_batch, dec_seq, a.shape[-1])
    q3 = per_seq(q_s)
    cache_kt = jnp.transpose(cache_k[0], (0, 2, 3, 4, 1)).reshape(n_pool, ATT_WIDTH, page)
    cache_vr = cache_v[0].reshape(n_pool, page * ATT_HEADS, V_DIM)
    pad8 = lambda a: _pad_to(per_seq(a), 1, SUBLANES)
    att_s, y_p = _decode_attn(page_table, jnp.concatenate([q3, q3], axis=1).astype(F32), cache_kt, cache_vr,
                              pad8(k_s), pad8(v_s), pad8(g_s), subln_w, lams, dec_seq, DECODE_PAGES,
                              (att_p, ssm_p[0], x_prompt[0], p_prompt[0, 0]), out_w)
    att_s = att_s[:, :dec_seq].reshape(n_rows, ATT_WIDTH).astype(BF16)
    ssm_s, conv_s, h_s = _ssd(per_seq(xbc_s), per_seq(dt_s), per_seq(z_s), state_conv[0],
                              state_ssm[0].reshape(dec_batch, SSM_WIDTH, D_STATE), ssd_w)
    y_s = _outproj(att_s, ssm_s.reshape(n_rows, SSM_WIDTH), x_sample.reshape(n_rows, D_MODEL),
                   p_sample[0].reshape(n_rows, PLE_DIM), out_w, tm=n_rows)

    hp = (SSM_HEADS, SSM_HEAD_DIM, D_STATE)
    return (y_p[None],
            y_s.reshape(dec_batch, dec_seq, D_MODEL),
            k_p.reshape(1, 1, seq, ATT_HEADS, 2, QK_DIM),
            v_p.reshape(1, 1, seq, ATT_HEADS, V_DIM),
            conv_p[None],
            h_p.reshape((1, 1) + hp),
            k_s.reshape(1, dec_batch, dec_seq, ATT_HEADS, 2, QK_DIM),
            v_s.reshape(1, dec_batch, dec_seq, ATT_HEADS, V_DIM),
            conv_s[None],
            h_s.reshape((1, dec_batch) + hp))
```

```python
import functools
import math

import jax
import jax.numpy as jnp
from jax import lax
from jax.experimental import pallas as pl
from jax.experimental.pallas import tpu as pltpu

F32 = jnp.float32
BF16 = jnp.bfloat16

D_MODEL = 1024
ATT_HEADS = 8
QK_DIM = 64
V_DIM = 2 * QK_DIM
ATT_WIDTH = ATT_HEADS * V_DIM
SM_SCALE = QK_DIM ** -0.5
LOG2E = math.log2(math.e)
ROPE_THETA = 10000.0
SSM_WIDTH = 1024
SSM_HEAD_DIM = 64
SSM_HEADS = SSM_WIDTH // SSM_HEAD_DIM
SSM_GROUPS = 2
D_STATE = 128
CONV_W = 4
CONV_DIM = SSM_WIDTH + 2 * SSM_GROUPS * D_STATE
CHUNK = 128
PLE_DIM = 256
EPS = 1e-6
LAM_INIT = 0.8 - 0.6 * math.exp(-0.3 * 0)

LANES = 128
SUBLANES = 8
BF16_ROWS = 16
NEG_BIG = -1e30
VMEM_LIMIT = 56 * 1024 * 1024

INPROJ_TM = 256
ATTN_TQ = 512
ATTN_TK = 512
UNROLL = 4
HEADS_PER_STEP = 2
DECODE_PAGES = 16


def _cparams(n_axes):
    return pltpu.CompilerParams(dimension_semantics=("arbitrary",) * n_axes,
                                vmem_limit_bytes=VMEM_LIMIT)


def _silu(x):
    return x * jax.nn.sigmoid(x)


def _inproj_body(x_ref, wn_ref, wq_ref, wk_ref, wv_ref, wg_ref, wz_ref, wx_ref, wdt_ref,
                 cos_ref, sin_ref, q_ref, k_ref, v_ref, g_ref, z_ref, xbc_ref, dt_ref,
                 *attn_copies):
    x = x_ref[...]
    hn = x * lax.rsqrt(jnp.mean(x * x, axis=-1, keepdims=True) + EPS) * wn_ref[...]
    hn = hn.astype(BF16)
    cos = cos_ref[...]
    sin = sin_ref[...]
    lane = lax.broadcasted_iota(jnp.int32, cos.shape, 1)
    first_half = (lane % QK_DIM) < (QK_DIM // 2)

    def rope_chunks(t):
        for c in range(t.shape[1] // LANES):
            tc = t[:, c * LANES:(c + 1) * LANES]
            partner = jnp.where(first_half,
                                pltpu.roll(tc, LANES - QK_DIM // 2, 1),
                                pltpu.roll(tc, QK_DIM // 2, 1))
            yield c, tc * cos + partner * sin

    q = jnp.dot(hn, wq_ref[...], preferred_element_type=F32)
    for c, r in rope_chunks(q):
        r = r * (SM_SCALE * LOG2E)
        if attn_copies:
            q_ref[c] = r.astype(BF16)
        else:
            q_ref[:, c * LANES:(c + 1) * LANES] = r.astype(BF16)
    k = jnp.dot(hn, wk_ref[...], preferred_element_type=F32)
    for c, r in rope_chunks(k):
        k_ref[:, c * LANES:(c + 1) * LANES] = r
        if attn_copies:
            attn_copies[0][c] = r.astype(BF16)
    v = jnp.dot(hn, wv_ref[...], preferred_element_type=F32)
    v_ref[...] = v
    if attn_copies:
        attn_copies[1][...] = jnp.transpose(v).astype(BF16)
    g_ref[...] = jnp.dot(hn, wg_ref[...], preferred_element_type=F32)
    z_ref[...] = jnp.dot(hn, wz_ref[...], preferred_element_type=F32)
    xbc_ref[...] = jnp.dot(hn, wx_ref[...], preferred_element_type=F32)
    dt_ref[...] = jnp.dot(hn, wdt_ref[...], preferred_element_type=F32)


def _inproj(x, w, cos, sin, tm, attn_tk=None):
    rows = x.shape[0]
    row = lambda n: pl.BlockSpec((tm, n), lambda i: (i, 0))
    wspec = lambda a: pl.BlockSpec(a.shape, lambda i: (0, 0), pipeline_mode=pl.Buffered(1))
    q_shape = (ATT_HEADS, rows, V_DIM) if attn_tk is not None else (rows, ATT_WIDTH)
    out_shapes = [
        jax.ShapeDtypeStruct(q_shape, BF16),
        jax.ShapeDtypeStruct((rows, ATT_WIDTH), F32),
        jax.ShapeDtypeStruct((rows, ATT_WIDTH), F32),
        jax.ShapeDtypeStruct((rows, ATT_WIDTH), F32),
        jax.ShapeDtypeStruct((rows, SSM_WIDTH), F32),
        jax.ShapeDtypeStruct((rows, CONV_DIM), F32),
        jax.ShapeDtypeStruct((rows, LANES), F32),
    ]
    out_specs = [row(s.shape[1]) for s in out_shapes]
    if attn_tk is not None:
        per_tile = attn_tk // tm
        assert per_tile * tm == attn_tk and rows % attn_tk == 0
        out_shapes += [jax.ShapeDtypeStruct((ATT_HEADS, rows, V_DIM), BF16),
                       jax.ShapeDtypeStruct((rows // attn_tk, ATT_WIDTH, attn_tk), BF16)]
        out_specs[0] = pl.BlockSpec((ATT_HEADS, tm, V_DIM), lambda i: (0, i, 0))
        out_specs += [pl.BlockSpec((ATT_HEADS, tm, V_DIM), lambda i: (0, i, 0)),
                      pl.BlockSpec((None, ATT_WIDTH, tm), lambda i: (i // per_tile, 0, i % per_tile))]
    return pl.pallas_call(
        _inproj_body,
        grid=(rows // tm,),
        in_specs=[row(D_MODEL), wspec(w["norm"]), wspec(w["q"]), wspec(w["k"]), wspec(w["v"]),
                  wspec(w["g"]), wspec(w["z"]), wspec(w["xbc"]), wspec(w["dt"]),
                  row(LANES), row(LANES)],
        out_specs=out_specs,
        out_shape=out_shapes,
        compiler_params=_cparams(1),
        name="inproj",
    )(x, w["norm"], w["q"], w["k"], w["v"], w["g"], w["z"], w["xbc"], w["dt"], cos, sin)


def _lambda_value(lq1_ref, lk1_ref, lq2_ref, lk2_ref):
    a = jnp.sum(lq1_ref[...] * lk1_ref[...], axis=-1, keepdims=True)
    b = jnp.sum(lq2_ref[...] * lk2_ref[...], axis=-1, keepdims=True)
    return jnp.exp(a) - jnp.exp(b) + LAM_INIT


def _sub_norm_gate(o, sub_w, g):
    o = o * lax.rsqrt(jnp.mean(o * o, axis=-1, keepdims=True) + EPS) * sub_w * (1.0 - LAM_INIT)
    return o * _silu(g)


def _prompt_attn_body(q_ref, k_ref, vt_ref, g_ref, sub_ref, lq1_ref, lk1_ref, lq2_ref, lk2_ref,
                      o_ref, m_ref, l_ref, acc_ref, s_ref, mx_ref, bias_ref, *, tq, tk):
    i = pl.program_id(1)

    @pl.when((pl.program_id(0) == 0) & (i == 0))
    def _():
        r = lax.broadcasted_iota(jnp.int32, bias_ref.shape, 0)
        c = lax.broadcasted_iota(jnp.int32, bias_ref.shape, 1) % tq
        bias_ref[...] = jnp.where(r <= c, 0.0, NEG_BIG)

    m_ref[...] = jnp.full(m_ref.shape, NEG_BIG, F32)
    l_ref[...] = jnp.zeros(l_ref.shape, F32)
    acc_ref[...] = jnp.zeros(acc_ref.shape, F32)

    zero = jnp.zeros((QK_DIM, tq), BF16)
    q_blk = []
    for hh in range(HEADS_PER_STEP):
        qt = jnp.transpose(q_ref[hh].astype(F32)).astype(BF16)
        q_blk.append(jnp.concatenate([jnp.concatenate([qt[:QK_DIM], zero], axis=0),
                                      jnp.concatenate([zero, qt[QK_DIM:]], axis=0)], axis=1))

    def scores(hh, t):
        kt = k_ref[hh, pl.ds(pl.multiple_of(t * tk, tk), tk), :]
        s = jnp.dot(kt, q_blk[hh], preferred_element_type=F32)
        s_ref[hh] = s
        mx_ref[hh] = jnp.max(s, axis=0, keepdims=True)

    def update(hh, t, diagonal):
        s = s_ref[hh]
        if diagonal:
            s = s + bias_ref[...]
            mx = jnp.max(s, axis=0, keepdims=True)
        else:
            mx = mx_ref[hh]
        m_prev = m_ref[hh]
        m_new = jnp.maximum(m_prev, mx)
        alpha = jnp.exp2(m_prev - m_new)
        p = jnp.exp2(s - m_new).astype(BF16)
        v_ext = jnp.concatenate([vt_ref[t, hh * V_DIM:(hh + 1) * V_DIM, :], jnp.ones((BF16_ROWS, tk), BF16)],
                                axis=0)
        pv = jnp.dot(v_ext, p, preferred_element_type=F32)
        l_ref[hh] = alpha * l_ref[hh] + pv[V_DIM:V_DIM + 1]
        acc_ref[hh] = alpha * acc_ref[hh] + pv[:V_DIM]
        m_ref[hh] = m_new

    def run(t0, count):
        for k in range(count):
            scores(1, t0 + k)
            update(0, t0 + k, False)
            scores(0, t0 + k + 1)
            update(1, t0 + k, False)

    n_full = (i * tq) // tk
    scores(0, 0)

    def unrolled(j, carry):
        run(UNROLL * j, UNROLL)
        return carry

    lax.fori_loop(0, n_full // UNROLL, unrolled, 0)
    t_rem = (n_full // UNROLL) * UNROLL
    for rem in range(UNROLL):
        @pl.when(n_full % UNROLL == rem)
        def _(rem=rem):
            run(t_rem, rem)
            scores(1, n_full)
            update(0, n_full, True)
            update(1, n_full, True)

    lam = _lambda_value(lq1_ref, lk1_ref, lq2_ref, lk2_ref)
    for hh in range(HEADS_PER_STEP):
        cols = slice(hh * V_DIM, (hh + 1) * V_DIM)
        o_both = acc_ref[hh] * (1.0 / l_ref[hh])
        o = jnp.transpose(o_both[:, :tq] - lam * o_both[:, tq:])
        o_ref[:, cols] = _sub_norm_gate(o, sub_ref[...], g_ref[:, cols]).astype(o_ref.dtype)


def _prompt_attn(q, k, vt, g, sub_w, lams, tq, tk):
    seq = q.shape[1]
    hps = HEADS_PER_STEP
    assert tq == tk and seq % tq == 0 and vt.shape == (seq // tk, ATT_WIDTH, tk) and ATT_HEADS % hps == 0
    row_tile = pl.BlockSpec((tq, hps * V_DIM), lambda h, i: (i, h))
    q_tile = pl.BlockSpec((hps, tq, V_DIM), lambda h, i: (h, i, 0))
    k_all = pl.BlockSpec((hps, seq, V_DIM), lambda h, i: (h, 0, 0))
    vt_all = pl.BlockSpec((seq // tk, hps * V_DIM, tk), lambda h, i: (0, h, 0))
    small = lambda a: pl.BlockSpec(a.shape, lambda h, i: (0, 0))
    return pl.pallas_call(
        functools.partial(_prompt_attn_body, tq=tq, tk=tk),
        grid=(ATT_HEADS // hps, seq // tq),
        in_specs=[q_tile, k_all, vt_all, row_tile, small(sub_w)] + [small(a) for a in lams],
        out_specs=row_tile,
        out_shape=jax.ShapeDtypeStruct((seq, ATT_WIDTH), BF16),
        scratch_shapes=[pltpu.VMEM((hps, 1, 2 * tq), F32), pltpu.VMEM((hps, 1, 2 * tq), F32),
                        pltpu.VMEM((hps, V_DIM, 2 * tq), F32),
                        pltpu.VMEM((hps, tk, 2 * tq), F32), pltpu.VMEM((hps, 1, 2 * tq), F32),
                        pltpu.VMEM((tk, 2 * tq), F32)],
        compiler_params=_cparams(2),
        name="prompt_attn",
    )(q, k, vt, g, sub_w, *lams)


XROWS = ATT_HEADS * SUBLANES


N_OUTPROJ_IN = 9
OUTPROJ_EVERY = 2


def _decode_attn_body(pt_ref, *refs, n_pages, dec_seq, group):
    del pt_ref
    q_ref = refs[0]
    k_refs = refs[1:1 + group]
    v_refs = refs[1 + group:1 + 2 * group]
    rest = refs[1 + 2 * group:]
    (kn_ref, vn_ref, g_ref, sub_ref, lq1_ref, lk1_ref, lq2_ref, lk2_ref) = rest[:8]
    outproj_in = rest[8:8 + N_OUTPROJ_IN]
    (o_ref, y_ref, m_ref, l_ref, acc_ref, qblk_ref, kbuf_ref, vbuf_ref) = rest[8 + N_OUTPROJ_IN:]
    step = pl.program_id(1)
    page = LANES

    @pl.when(step == 0)
    def _():
        m_ref[...] = jnp.full(m_ref.shape, NEG_BIG, F32)
        l_ref[...] = jnp.zeros(l_ref.shape, F32)
        acc_ref[...] = jnp.zeros(acc_ref.shape, F32)
        q8 = q_ref[...]
        row = lax.broadcasted_iota(jnp.int32, q8.shape, 0)
        lane = lax.broadcasted_iota(jnp.int32, q8.shape, 1)
        own_map = ((lane % V_DIM) < QK_DIM) == (row < dec_seq)
        for h in range(ATT_HEADS):
            keep = own_map & ((lane // V_DIM) == h)
            qblk_ref[h * SUBLANES:(h + 1) * SUBLANES, :] = jnp.where(keep, q8, 0.0).astype(BF16)

    qblk = qblk_ref[...]
    first_of_pair = lax.broadcasted_iota(jnp.int32, (BF16_ROWS, V_DIM), 0) < SUBLANES

    def process(s, value_fn):
        m_prev = m_ref[...]
        m_new = jnp.maximum(m_prev, jnp.max(s, axis=-1, keepdims=True))
        alpha = jnp.exp2(m_prev - m_new)
        p = jnp.exp2(s - m_new)
        l_ref[...] = alpha * l_ref[...] + jnp.sum(p, axis=-1, keepdims=True)
        p = p.astype(BF16)
        pv_parts = []
        for pair in range(ATT_HEADS // 2):
            p_pair = p[pair * BF16_ROWS:(pair + 1) * BF16_ROWS]
            halves = [jnp.dot(p_pair, value_fn(h).astype(BF16), preferred_element_type=F32)
                      for h in (2 * pair, 2 * pair + 1)]
            pv_parts.append(jnp.where(first_of_pair, halves[0], halves[1]))
        acc_ref[...] = alpha * acc_ref[...] + jnp.concatenate(pv_parts, axis=0)
        m_ref[...] = m_new

    s_pages = jnp.concatenate([jnp.dot(qblk, kp[...].astype(BF16), preferred_element_type=F32)
                               for kp in k_refs], axis=1)
    process(s_pages, lambda h: jnp.concatenate(
        [vp[pl.ds(h, page, stride=ATT_HEADS), :] for vp in v_refs], axis=0))

    @pl.when(step == n_pages // group - 1)
    def _():
        kbuf_ref[...] = jnp.zeros(kbuf_ref.shape, F32)
        vbuf_ref[...] = jnp.zeros(vbuf_ref.shape, F32)
        kbuf_ref[0:SUBLANES, :] = kn_ref[...]
        vbuf_ref[0:SUBLANES, :] = vn_ref[...]
        s_new = lax.dot_general(qblk, kbuf_ref[...].astype(BF16), (((1,), (1,)), ((), ())),
                                preferred_element_type=F32)
        qi = lax.broadcasted_iota(jnp.int32, s_new.shape, 0) % dec_seq
        kj = lax.broadcasted_iota(jnp.int32, s_new.shape, 1)
        process(jnp.where(kj <= qi, s_new, NEG_BIG), lambda h: vbuf_ref[:, h * V_DIM:(h + 1) * V_DIM])

        t = acc_ref[...] / l_ref[...]
        lam = _lambda_value(lq1_ref, lk1_ref, lq2_ref, lk2_ref)
        g = g_ref[...]
        for h in range(ATT_HEADS):
            cols = slice(h * V_DIM, (h + 1) * V_DIM)
            th = t[h * SUBLANES:(h + 1) * SUBLANES]
            o = th - lam * pltpu.roll(th, SUBLANES - dec_seq, 0)
            o_ref[:, cols] = _sub_norm_gate(o, sub_ref[...], g[:, cols])

    @pl.when((pl.program_id(0) * (n_pages // group) + step) % OUTPROJ_EVERY == 0)
    def _():
        _outproj_body(*outproj_in, y_ref)


def _decode_attn(page_table, q8, cache_kt, cache_vr, k_new8, v_new8, g8, sub_w, lams, dec_seq, group,
                 outproj_rows, outproj_w):
    batch, n_pages = page_table.shape
    page = cache_kt.shape[2]
    assert page == LANES and 2 * dec_seq == SUBLANES and n_pages % group == 0
    steps = n_pages // group
    rows = outproj_rows[2].shape[0]
    n_tiles = batch * steps // OUTPROJ_EVERY
    tm = rows // n_tiles
    assert tm * n_tiles == rows and n_tiles * OUTPROJ_EVERY == batch * steps and tm % BF16_ROWS == 0
    row_tile = lambda a: pl.BlockSpec((tm, a.shape[1]), lambda b, p, pt: ((b * steps + p) // OUTPROJ_EVERY, 0))
    weight = lambda a: pl.BlockSpec(a.shape, lambda b, p, pt: (0, 0), pipeline_mode=pl.Buffered(1))
    outproj_ws = _outproj_weights(outproj_w)
    per_b = lambda a: pl.BlockSpec((None,) + a.shape[1:], lambda b, p, pt: (b, 0, 0))
    paged = lambda a, n: pl.BlockSpec((None,) + a.shape[1:],
                                      lambda b, p, pt: (pt[b, p * group + n], 0, 0))
    small = lambda a: pl.BlockSpec(a.shape, lambda b, p, pt: (0, 0))
    grid_spec = pltpu.PrefetchScalarGridSpec(
        num_scalar_prefetch=1,
        grid=(batch, n_pages // group),
        in_specs=[per_b(q8)] + [paged(cache_kt, n) for n in range(group)]
                 + [paged(cache_vr, n) for n in range(group)]
                 + [per_b(k_new8), per_b(v_new8), per_b(g8), small(sub_w)]
                 + [small(a) for a in lams]
                 + [row_tile(a) for a in outproj_rows] + [weight(a) for a in outproj_ws],
        out_specs=[per_b(g8), row_tile(outproj_rows[2])],
        scratch_shapes=[pltpu.VMEM((XROWS, 1), F32), pltpu.VMEM((XROWS, 1), F32),
                        pltpu.VMEM((XROWS, V_DIM), F32), pltpu.VMEM((XROWS, ATT_WIDTH), BF16),
                        pltpu.VMEM((page, ATT_WIDTH), F32), pltpu.VMEM((page, ATT_WIDTH), F32)],
    )
    return pl.pallas_call(
        functools.partial(_decode_attn_body, n_pages=n_pages, dec_seq=dec_seq, group=group),
        grid_spec=grid_spec,
        out_shape=(jax.ShapeDtypeStruct(g8.shape, F32), jax.ShapeDtypeStruct(outproj_rows[2].shape, F32)),
        compiler_params=_cparams(2),
        name="decode_attn_outproj",
    )(page_table, q8, *([cache_kt] * group), *([cache_vr] * group), k_new8, v_new8, g8, sub_w, *lams,
      *outproj_rows, *outproj_ws)


HEADS_PER_GROUP = SSM_HEADS // SSM_GROUPS
GROUP_WIDTH = SSM_WIDTH // SSM_GROUPS
TAIL = SUBLANES


def _pair_expand(a, j):
    lane = lax.broadcasted_iota(jnp.int32, (a.shape[0], LANES), 1)
    return jnp.where(lane < SSM_HEAD_DIM, a[:, 2 * j:2 * j + 1], a[:, 2 * j + 1:2 * j + 2])


def _ssd_body(xbc_ref, dt_ref, z_ref, cprev_ref, h0_ref, cw_ref, cb_ref, dtb_ref, alog_ref,
              dskip_ref, gn_ref,
              y_ref, cnew_ref, hnew_ref,
              xs_ref, dtp_ref, ht_ref, *, valid, n_chunks):
    q = CHUNK
    c = pl.program_id(1)
    first_step = (pl.program_id(0) == 0) & (c == 0)

    @pl.when(first_step)
    def _():
        xs_ref[...] = jnp.zeros(xs_ref.shape, F32)
        dtp_ref[...] = jnp.zeros(dtp_ref.shape, F32)

    @pl.when(c == 0)
    def _():
        xs_ref[TAIL - (CONV_W - 1):TAIL, :] = cprev_ref[...]
        ht_ref[...] = jnp.transpose(h0_ref[...])

    xs_ref[TAIL:TAIL + valid, :] = xbc_ref[...]
    dtp_ref[0:valid, :] = dt_ref[...]

    u = cb_ref[...]
    for w in range(CONV_W):
        u = u + xs_ref[pl.ds(TAIL - (CONV_W - 1) + w, q), :] * cw_ref[w:w + 1, :]
    u = _silu(u)

    @pl.when(c == n_chunks - 1)
    def _():
        cnew_ref[...] = xs_ref[pl.ds(TAIL + valid - (CONV_W - 1), CONV_W - 1), :]

    if n_chunks > 1:
        xs_ref[0:TAIL, :] = xs_ref[q:q + TAIL, :]

    xs = u[:, :SSM_WIDTH]
    bm = u[:, SSM_WIDTH:SSM_WIDTH + SSM_GROUPS * D_STATE]
    cm = u[:, SSM_WIDTH + SSM_GROUPS * D_STATE:]

    row = lax.broadcasted_iota(jnp.int32, (q, LANES), 0)
    x_dt = dtp_ref[...] + dtb_ref[...]
    dt = jnp.maximum(x_dt, 0.0) + jnp.log1p(jnp.exp(-jnp.abs(x_dt)))
    if valid < q:
        dt = jnp.where(row < valid, dt, 0.0)
    a = -jnp.exp(alog_ref[...])
    da = dt * a

    tt = lax.broadcasted_iota(jnp.int32, (q, q), 0)
    ss = lax.broadcasted_iota(jnp.int32, (q, q), 1)
    causal = ss <= tt
    tril = jnp.where(causal, 1.0, 0.0).astype(BF16)
    da_hi = da.astype(BF16)
    r1 = da - da_hi.astype(F32)
    da_mid = r1.astype(BF16)
    da_lo = (r1 - da_mid.astype(F32)).astype(BF16)
    cs3 = jnp.dot(tril, jnp.concatenate([da_hi, da_mid, da_lo], axis=1), preferred_element_type=F32)
    cs = cs3[:, :LANES] + cs3[:, LANES:2 * LANES] + cs3[:, 2 * LANES:]

    cs_t = jnp.transpose(cs)
    dt_t = jnp.transpose(dt)
    cs_last = cs[q - 1:q, :]
    e_cs = jnp.exp(cs)
    dec_end = jnp.exp(cs_last - cs) * dt
    chunk_decay = jnp.exp(cs_last)

    ht = ht_ref[...]
    ht_b = ht.astype(BF16)
    y_parts = []
    st_parts = []
    for grp in range(SSM_GROUPS):
        bg = bm[:, grp * D_STATE:(grp + 1) * D_STATE]
        cg = cm[:, grp * D_STATE:(grp + 1) * D_STATE].astype(BF16)
        cb = lax.dot_general(cg, bg.astype(BF16), (((1,), (1,)), ((), ())),
                             preferred_element_type=F32)
        y_off = jnp.dot(cg, ht_b[:, grp * GROUP_WIDTH:(grp + 1) * GROUP_WIDTH],
                        preferred_element_type=F32)
        xd_parts = []
        for jp in range(HEADS_PER_GROUP // 2):
            j = grp * (HEADS_PER_GROUP // 2) + jp
            x_pair = xs[:, j * LANES:(j + 1) * LANES]
            x_pair_b = x_pair.astype(BF16)
            y_pair = []
            for h in (2 * j, 2 * j + 1):
                diff = cs[:, h:h + 1] - cs_t[h:h + 1, :]
                lmat = jnp.exp(jnp.where(causal, diff, NEG_BIG))
                wmat = (cb * lmat * dt_t[h:h + 1, :]).astype(BF16)
                y_pair.append(jnp.dot(wmat, x_pair_b, preferred_element_type=F32))
            lane = lax.broadcasted_iota(jnp.int32, (q, LANES), 1)
            y_diag = jnp.where(lane < SSM_HEAD_DIM, y_pair[0], y_pair[1])
            y_parts.append(y_diag + y_off[:, jp * LANES:(jp + 1) * LANES] * _pair_expand(e_cs, j))
            xd_parts.append((x_pair * _pair_expand(dec_end, j)).astype(BF16))
        xd = jnp.concatenate(xd_parts, axis=1)
        st_parts.append(jnp.dot(jnp.transpose(bg).astype(BF16), xd, preferred_element_type=F32))
    decay_lanes = jnp.concatenate([_pair_expand(chunk_decay, j) for j in range(SSM_HEADS // 2)], axis=1)
    ht_new = ht * decay_lanes + jnp.concatenate(st_parts, axis=1)
    ht_ref[...] = ht_new

    @pl.when(c == n_chunks - 1)
    def _():
        hnew_ref[...] = jnp.transpose(ht_new)

    y = jnp.concatenate(y_parts, axis=1) + dskip_ref[...] * xs
    y = y[0:valid] * _silu(z_ref[...])
    outs = []
    for grp in range(SSM_GROUPS):
        yg = y[:, grp * GROUP_WIDTH:(grp + 1) * GROUP_WIDTH]
        outs.append(yg * lax.rsqrt(jnp.mean(yg * yg, axis=-1, keepdims=True) + EPS))
    y_ref[...] = (jnp.concatenate(outs, axis=1) * gn_ref[...]).astype(y_ref.dtype)


def _ssd(xbc, dt_raw, z, conv_prev, h0, w):
    n_seq, rows, _ = xbc.shape
    valid = min(rows, CHUNK)
    n_chunks = rows // valid
    assert n_chunks * valid == rows and valid >= CONV_W - 1
    blk = lambda n: pl.BlockSpec((None, valid, n), lambda s, c: (s, c, 0))
    per_seq = lambda a: pl.BlockSpec((None,) + a.shape[1:], lambda s, c: (s, 0, 0))
    small = lambda a: pl.BlockSpec(a.shape, lambda s, c: (0, 0))
    consts = [w["conv_w"], w["conv_b"], w["dt_bias"], w["a_log"], w["d_skip"], w["gnorm"]]
    return pl.pallas_call(
        functools.partial(_ssd_body, valid=valid, n_chunks=n_chunks),
        grid=(n_seq, n_chunks),
        in_specs=[blk(CONV_DIM), blk(LANES), blk(SSM_WIDTH), per_seq(conv_prev), per_seq(h0)]
                 + [small(a) for a in consts],
        out_specs=[blk(SSM_WIDTH), per_seq(conv_prev), per_seq(h0)],
        out_shape=(jax.ShapeDtypeStruct((n_seq, rows, SSM_WIDTH), BF16),
                   jax.ShapeDtypeStruct(conv_prev.shape, F32),
                   jax.ShapeDtypeStruct(h0.shape, F32)),
        scratch_shapes=[pltpu.VMEM((CHUNK + TAIL, CONV_DIM), F32),
                        pltpu.VMEM((CHUNK, LANES), F32),
                        pltpu.VMEM((D_STATE, SSM_WIDTH), F32)],
        compiler_params=_cparams(2),
        name="ssd",
    )(xbc, dt_raw, z, conv_prev, h0, *consts)


def _outproj_body(att_ref, ssm_ref, x_ref, p_ref, woa_ref, wos_ref, wpp_ref, wpg_ref, fn_ref, y_ref):
    hmid = (x_ref[...]
            + jnp.dot(att_ref[...], woa_ref[...], preferred_element_type=F32)
            + jnp.dot(ssm_ref[...], wos_ref[...], preferred_element_type=F32))
    emb = jnp.dot(p_ref[...].astype(BF16), wpp_ref[...], preferred_element_type=F32)
    gate = jax.nn.sigmoid(jnp.dot(hmid.astype(BF16), wpg_ref[...], preferred_element_type=F32))
    out = hmid + emb * gate
    y_ref[...] = out * lax.rsqrt(jnp.mean(out * out, axis=-1, keepdims=True) + EPS) * fn_ref[...]


def _outproj_weights(w):
    return [w["out_att"], w["out_ssm"], w["ple_proj"], w["ple_gate"], w["final_norm"]]


def _outproj(att, ssm, x, p, w, tm):
    rows = x.shape[0]
    row = lambda n: pl.BlockSpec((tm, n), lambda i: (i, 0))
    wspec = lambda a: pl.BlockSpec(a.shape, lambda i: (0, 0))
    ws = _outproj_weights(w)
    return pl.pallas_call(
        _outproj_body,
        grid=(rows // tm,),
        in_specs=[row(ATT_WIDTH), row(SSM_WIDTH), row(D_MODEL), row(PLE_DIM)] + [wspec(a) for a in ws],
        out_specs=row(D_MODEL),
        out_shape=jax.ShapeDtypeStruct((rows, D_MODEL), F32),
        compiler_params=_cparams(1),
        name="outproj",
    )(att, ssm, x, p, *ws)


def _rope_tables(pos):
    half = QK_DIM // 2
    inv = ROPE_THETA ** (-jnp.arange(0, QK_DIM, 2, dtype=F32) / QK_DIM)
    ang = pos.astype(F32)[:, None] * inv[None, :]
    cos = jnp.tile(jnp.cos(ang), (1, LANES // half))
    sin = jnp.sin(ang)
    sin = jnp.tile(jnp.concatenate([-sin, sin], axis=1), (1, LANES // QK_DIM))
    return cos, sin


def _pad_to(a, axis, size):
    pads = [(0, 0)] * a.ndim
    pads[axis] = (0, size - a.shape[axis])
    return jnp.pad(a, pads)


def kernel(x_prompt, x_sample, cache_k, cache_v, state_conv, state_ssm, page_table, p_prompt, p_sample, w_norm, w_in, lambda_q1, lambda_k1, lambda_q2, lambda_k2, subln_w, conv_w, conv_b, dt_bias, A_log, D_skip, gnorm_w, w_out, w_ple_proj, w_ple_gate, final_norm_w):
    assert w_norm.shape[0] == 1 and x_prompt.shape[0] == 1
    seq = x_prompt.shape[1]
    dec_batch, dec_seq, _ = x_sample.shape
    n_pool, page = cache_k.shape[1], cache_k.shape[2]
    past = page_table.shape[1] * page

    w_in0 = w_in[0]
    splits = [0, 1024, 2048, 3072, 4096, 5120, 5120 + CONV_DIM, 5120 + CONV_DIM + SSM_HEADS]
    names = ["q", "k", "v", "g", "z", "xbc", "dt"]
    w = {n: w_in0[:, a:b].astype(BF16) for n, a, b in zip(names, splits[:-1], splits[1:])}
    w["dt"] = _pad_to(w["dt"], 1, LANES)
    w["norm"] = w_norm
    ssd_w = {"conv_w": conv_w[0], "conv_b": conv_b, "dt_bias": _pad_to(dt_bias, 1, LANES),
             "a_log": _pad_to(A_log, 1, LANES), "d_skip": jnp.repeat(D_skip, SSM_HEAD_DIM, axis=1),
             "gnorm": gnorm_w}
    out_w = {"out_att": w_out[0, :ATT_WIDTH].astype(BF16), "out_ssm": w_out[0, ATT_WIDTH:].astype(BF16),
             "ple_proj": w_ple_proj[0].astype(BF16), "ple_gate": w_ple_gate[0].astype(BF16),
             "final_norm": final_norm_w[None, :]}
    lams = [lambda_q1, lambda_k1, lambda_q2, lambda_k2]

    cos_p, sin_p = _rope_tables(jnp.arange(seq))
    q_p, k_p, v_p, g_p, z_p, xbc_p, dt_p, kb_p, vt_p = _inproj(
        x_prompt[0], w, cos_p, sin_p, tm=INPROJ_TM, attn_tk=ATTN_TK)
    att_p = _prompt_attn(q_p, kb_p, vt_p, g_p, subln_w, lams, tq=ATTN_TQ, tk=ATTN_TK)
    ssm_p, conv_p, h_p = _ssd(xbc_p[None], dt_p[None], z_p[None],
                              jnp.zeros((1, CONV_W - 1, CONV_DIM), F32),
                              jnp.zeros((1, SSM_WIDTH, D_STATE), F32), ssd_w)

    n_rows = dec_batch * dec_seq
    cos_s, sin_s = _rope_tables(past + jnp.arange(n_rows) % dec_seq)
    q_s, k_s, v_s, g_s, z_s, xbc_s, dt_s = _inproj(x_sample.reshape(n_rows, D_MODEL), w,
                                                 cos_s, sin_s, tm=n_rows)
    per_seq = lambda a: a.reshape(dec_batch, dec_seq, a.shape[-1])
    q3 = per_seq(q_s)
    cache_kt = jnp.transpose(cache_k[0], (0, 2, 3, 4, 1)).reshape(n_pool, ATT_WIDTH, page)
    cache_vr = cache_v[0].reshape(n_pool, page * ATT_HEADS, V_DIM)
    pad8 = lambda a: _pad_to(per_seq(a), 1, SUBLANES)
    att_s, y_p = _decode_attn(page_table, jnp.concatenate([q3, q3], axis=1).astype(F32), cache_kt, cache_vr,
                              pad8(k_s), pad8(v_s), pad8(g_s), subln_w, lams, dec_seq, DECODE_PAGES,
                              (att_p, ssm_p[0], x_prompt[0], p_prompt[0, 0]), out_w)
    att_s = att_s[:, :dec_seq].reshape(n_rows, ATT_WIDTH).astype(BF16)
    ssm_s, conv_s, h_s = _ssd(per_seq(xbc_s), per_seq(dt_s), per_seq(z_s), state_conv[0],
                              state_ssm[0].reshape(dec_batch, SSM_WIDTH, D_STATE), ssd_w)
    y_s = _outproj(att_s, ssm_s.reshape(n_rows, SSM_WIDTH), x_sample.reshape(n_rows, D_MODEL),
                   p_sample[0].reshape(n_rows, PLE_DIM), out_w, tm=n_rows)

    hp = (SSM_HEADS, SSM_HEAD_DIM, D_STATE)
    return (y_p[None],
            y_s.reshape(dec_batch, dec_seq, D_MODEL),
            k_p.reshape(1, 1, seq, ATT_HEADS, 2, QK_DIM),
            v_p.reshape(1, 1, seq, ATT_HEADS, V_DIM),
            conv_p[None],
            h_p.reshape((1, 1) + hp),
            k_s.reshape(1, dec_batch, dec_seq, ATT_HEADS, 2, QK_DIM),
            v_s.reshape(1, dec_batch, dec_seq, ATT_HEADS, V_DIM),
            conv_s[None],
            h_s.reshape((1, dec_batch) + hp))
```

```python
import functools
import math

import jax
import jax.numpy as jnp
from jax import lax
from jax.experimental import pallas as pl
from jax.experimental.pallas import tpu as pltpu

F32 = jnp.float32
BF16 = jnp.bfloat16

D_MODEL = 1024
ATT_HEADS = 8
QK_DIM = 64
V_DIM = 2 * QK_DIM
ATT_WIDTH = ATT_HEADS * V_DIM
SM_SCALE = QK_DIM ** -0.5
LOG2E = math.log2(math.e)
ROPE_THETA = 10000.0
SSM_WIDTH = 1024
SSM_HEAD_DIM = 64
SSM_HEADS = SSM_WIDTH // SSM_HEAD_DIM
SSM_GROUPS = 2
D_STATE = 128
CONV_W = 4
CONV_DIM = SSM_WIDTH + 2 * SSM_GROUPS * D_STATE
CHUNK = 128
PLE_DIM = 256
EPS = 1e-6
LAM_INIT = 0.8 - 0.6 * math.exp(-0.3 * 0)

LANES = 128
SUBLANES = 8
BF16_ROWS = 16
NEG_BIG = -1e30
VMEM_LIMIT = 56 * 1024 * 1024

INPROJ_TM = 256
ATTN_TQ = 512
ATTN_TK = 512
UNROLL = 2
HEADS_PER_STEP = 4
DECODE_PAGES = 16


def _cparams(n_axes):
    return pltpu.CompilerParams(dimension_semantics=("arbitrary",) * n_axes,
                                vmem_limit_bytes=VMEM_LIMIT)


def _silu(x):
    return x * jax.nn.sigmoid(x)


def _inproj_body(x_ref, wn_ref, wq_ref, wk_ref, wv_ref, wg_ref, wz_ref, wx_ref, wdt_ref,
                 cos_ref, sin_ref, q_ref, k_ref, v_ref, g_ref, z_ref, xbc_ref, dt_ref,
                 *attn_copies):
    x = x_ref[...]
    hn = x * lax.rsqrt(jnp.mean(x * x, axis=-1, keepdims=True) + EPS) * wn_ref[...]
    hn = hn.astype(BF16)
    cos = cos_ref[...]
    sin = sin_ref[...]
    lane = lax.broadcasted_iota(jnp.int32, cos.shape, 1)
    first_half = (lane % QK_DIM) < (QK_DIM // 2)

    def rope_chunks(t):
        for c in range(t.shape[1] // LANES):
            tc = t[:, c * LANES:(c + 1) * LANES]
            partner = jnp.where(first_half,
                                pltpu.roll(tc, LANES - QK_DIM // 2, 1),
                                pltpu.roll(tc, QK_DIM // 2, 1))
            yield c, tc * cos + partner * sin

    q = jnp.dot(hn, wq_ref[...], preferred_element_type=F32)
    for c, r in rope_chunks(q):
        r = r * (SM_SCALE * LOG2E)
        if attn_copies:
            q_ref[c] = r.astype(BF16)
        else:
            q_ref[:, c * LANES:(c + 1) * LANES] = r.astype(BF16)
    k = jnp.dot(hn, wk_ref[...], preferred_element_type=F32)
    for c, r in rope_chunks(k):
        k_ref[:, c * LANES:(c + 1) * LANES] = r
        if attn_copies:
            attn_copies[0][c] = r.astype(BF16)
    v = jnp.dot(hn, wv_ref[...], preferred_element_type=F32)
    v_ref[...] = v
    if attn_copies:
        attn_copies[1][...] = jnp.transpose(v).astype(BF16)
    g_ref[...] = jnp.dot(hn, wg_ref[...], preferred_element_type=F32)
    z_ref[...] = jnp.dot(hn, wz_ref[...], preferred_element_type=F32)
    xbc_ref[...] = jnp.dot(hn, wx_ref[...], preferred_element_type=F32)
    dt_ref[...] = jnp.dot(hn, wdt_ref[...], preferred_element_type=F32)


def _inproj(x, w, cos, sin, tm, attn_tk=None):
    rows = x.shape[0]
    row = lambda n: pl.BlockSpec((tm, n), lambda i: (i, 0))
    wspec = lambda a: pl.BlockSpec(a.shape, lambda i: (0, 0), pipeline_mode=pl.Buffered(1))
    q_shape = (ATT_HEADS, rows, V_DIM) if attn_tk is not None else (rows, ATT_WIDTH)
    out_shapes = [
        jax.ShapeDtypeStruct(q_shape, BF16),
        jax.ShapeDtypeStruct((rows, ATT_WIDTH), F32),
        jax.ShapeDtypeStruct((rows, ATT_WIDTH), F32),
        jax.ShapeDtypeStruct((rows, ATT_WIDTH), F32),
        jax.ShapeDtypeStruct((rows, SSM_WIDTH), F32),
        jax.ShapeDtypeStruct((rows, CONV_DIM), F32),
        jax.ShapeDtypeStruct((rows, LANES), F32),
    ]
    out_specs = [row(s.shape[1]) for s in out_shapes]
    if attn_tk is not None:
        per_tile = attn_tk // tm
        assert per_tile * tm == attn_tk and rows % attn_tk == 0
        out_shapes += [jax.ShapeDtypeStruct((ATT_HEADS, rows, V_DIM), BF16),
                       jax.ShapeDtypeStruct((rows // attn_tk, ATT_WIDTH, attn_tk), BF16)]
        out_specs[0] = pl.BlockSpec((ATT_HEADS, tm, V_DIM), lambda i: (0, i, 0))
        out_specs += [pl.BlockSpec((ATT_HEADS, tm, V_DIM), lambda i: (0, i, 0)),
                      pl.BlockSpec((None, ATT_WIDTH, tm), lambda i: (i // per_tile, 0, i % per_tile))]
    return pl.pallas_call(
        _inproj_body,
        grid=(rows // tm,),
        in_specs=[row(D_MODEL), wspec(w["norm"]), wspec(w["q"]), wspec(w["k"]), wspec(w["v"]),
                  wspec(w["g"]), wspec(w["z"]), wspec(w["xbc"]), wspec(w["dt"]),
                  row(LANES), row(LANES)],
        out_specs=out_specs,
        out_shape=out_shapes,
        compiler_params=_cparams(1),
        name="inproj",
    )(x, w["norm"], w["q"], w["k"], w["v"], w["g"], w["z"], w["xbc"], w["dt"], cos, sin)


def _lambda_value(lq1_ref, lk1_ref, lq2_ref, lk2_ref):
    a = jnp.sum(lq1_ref[...] * lk1_ref[...], axis=-1, keepdims=True)
    b = jnp.sum(lq2_ref[...] * lk2_ref[...], axis=-1, keepdims=True)
    return jnp.exp(a) - jnp.exp(b) + LAM_INIT


def _sub_norm_gate(o, sub_w, g):
    o = o * lax.rsqrt(jnp.mean(o * o, axis=-1, keepdims=True) + EPS) * sub_w * (1.0 - LAM_INIT)
    return o * _silu(g)


def _prompt_attn_body(q_ref, k_ref, vt_ref, g_ref, sub_ref, lq1_ref, lk1_ref, lq2_ref, lk2_ref,
                      o_ref, m_ref, l_ref, acc_ref, s_ref, mx_ref, bias_ref, *, tq, tk):
    i = pl.program_id(1)

    @pl.when((pl.program_id(0) == 0) & (i == 0))
    def _():
        r = lax.broadcasted_iota(jnp.int32, bias_ref.shape, 0)
        c = lax.broadcasted_iota(jnp.int32, bias_ref.shape, 1) % tq
        bias_ref[...] = jnp.where(r <= c, 0.0, NEG_BIG)

    m_ref[...] = jnp.full(m_ref.shape, NEG_BIG, F32)
    l_ref[...] = jnp.zeros(l_ref.shape, F32)
    acc_ref[...] = jnp.zeros(acc_ref.shape, F32)

    zero = jnp.zeros((QK_DIM, tq), BF16)
    q_blk = []
    for hh in range(HEADS_PER_STEP):
        qt = jnp.transpose(q_ref[hh].astype(F32)).astype(BF16)
        q_blk.append(jnp.concatenate([jnp.concatenate([qt[:QK_DIM], zero], axis=0),
                                      jnp.concatenate([zero, qt[QK_DIM:]], axis=0)], axis=1))

    def scores(hh, t):
        kt = k_ref[hh, pl.ds(pl.multiple_of(t * tk, tk), tk), :]
        s = jnp.dot(kt, q_blk[hh], preferred_element_type=F32)
        s_ref[hh] = s
        mx_ref[hh] = jnp.max(s, axis=0, keepdims=True)

    def update(hh, t, diagonal):
        s = s_ref[hh]
        if diagonal:
            s = s + bias_ref[...]
            mx = jnp.max(s, axis=0, keepdims=True)
        else:
            mx = mx_ref[hh]
        m_prev = m_ref[hh]
        m_new = jnp.maximum(m_prev, mx)
        alpha = jnp.exp2(m_prev - m_new)
        p = jnp.exp2(s - m_new).astype(BF16)
        v_ext = jnp.concatenate([vt_ref[t, hh * V_DIM:(hh + 1) * V_DIM, :], jnp.ones((BF16_ROWS, tk), BF16)],
                                axis=0)
        pv = jnp.dot(v_ext, p, preferred_element_type=F32)
        l_ref[hh] = alpha * l_ref[hh] + pv[V_DIM:V_DIM + 1]
        acc_ref[hh] = alpha * acc_ref[hh] + pv[:V_DIM]
        m_ref[hh] = m_new

    def run(t0, count):
        for k in range(count):
            for hh in range(HEADS_PER_STEP):
                nxt = (hh + 1) % HEADS_PER_STEP
                scores(nxt, t0 + k + (1 if nxt == 0 else 0))
                update(hh, t0 + k, False)

    n_full = (i * tq) // tk
    scores(0, 0)

    def unrolled(j, carry):
        run(UNROLL * j, UNROLL)
        return carry

    lax.fori_loop(0, n_full // UNROLL, unrolled, 0)
    t_rem = (n_full // UNROLL) * UNROLL
    for rem in range(UNROLL):
        @pl.when(n_full % UNROLL == rem)
        def _(rem=rem):
            run(t_rem, rem)
            for hh in range(HEADS_PER_STEP):
                if hh + 1 < HEADS_PER_STEP:
                    scores(hh + 1, n_full)
                update(hh, n_full, True)

    lam = _lambda_value(lq1_ref, lk1_ref, lq2_ref, lk2_ref)
    for hh in range(HEADS_PER_STEP):
        cols = slice(hh * V_DIM, (hh + 1) * V_DIM)
        o_both = acc_ref[hh] * (1.0 / l_ref[hh])
        o = jnp.transpose(o_both[:, :tq] - lam * o_both[:, tq:])
        o_ref[:, cols] = _sub_norm_gate(o, sub_ref[...], g_ref[:, cols]).astype(o_ref.dtype)


def _prompt_attn(q, k, vt, g, sub_w, lams, tq, tk):
    seq = q.shape[1]
    hps = HEADS_PER_STEP
    assert tq == tk and seq % tq == 0 and vt.shape == (seq // tk, ATT_WIDTH, tk) and ATT_HEADS % hps == 0
    row_tile = pl.BlockSpec((tq, hps * V_DIM), lambda h, i: (i, h))
    q_tile = pl.BlockSpec((hps, tq, V_DIM), lambda h, i: (h, i, 0))
    once = pl.Buffered(1)
    k_all = pl.BlockSpec((hps, seq, V_DIM), lambda h, i: (h, 0, 0), pipeline_mode=once)
    vt_all = pl.BlockSpec((seq // tk, hps * V_DIM, tk), lambda h, i: (0, h, 0), pipeline_mode=once)
    small = lambda a: pl.BlockSpec(a.shape, lambda h, i: (0, 0))
    return pl.pallas_call(
        functools.partial(_prompt_attn_body, tq=tq, tk=tk),
        grid=(ATT_HEADS // hps, seq // tq),
        in_specs=[q_tile, k_all, vt_all, row_tile, small(sub_w)] + [small(a) for a in lams],
        out_specs=row_tile,
        out_shape=jax.ShapeDtypeStruct((seq, ATT_WIDTH), BF16),
        scratch_shapes=[pltpu.VMEM((hps, 1, 2 * tq), F32), pltpu.VMEM((hps, 1, 2 * tq), F32),
                        pltpu.VMEM((hps, V_DIM, 2 * tq), F32),
                        pltpu.VMEM((hps, tk, 2 * tq), F32), pltpu.VMEM((hps, 1, 2 * tq), F32),
                        pltpu.VMEM((tk, 2 * tq), F32)],
        compiler_params=_cparams(2),
        name="prompt_attn",
    )(q, k, vt, g, sub_w, *lams)


XROWS = ATT_HEADS * SUBLANES


N_OUTPROJ_IN = 9
OUTPROJ_EVERY = 2


def _decode_attn_body(pt_ref, *refs, n_pages, dec_seq, group):
    del pt_ref
    q_ref = refs[0]
    k_refs = refs[1:1 + group]
    v_refs = refs[1 + group:1 + 2 * group]
    rest = refs[1 + 2 * group:]
    (kn_ref, vn_ref, g_ref, sub_ref, lq1_ref, lk1_ref, lq2_ref, lk2_ref) = rest[:8]
    outproj_in = rest[8:8 + N_OUTPROJ_IN]
    (o_ref, y_ref, m_ref, l_ref, acc_ref, qblk_ref, kbuf_ref, vbuf_ref) = rest[8 + N_OUTPROJ_IN:]
    step = pl.program_id(1)
    page = LANES

    @pl.when(step == 0)
    def _():
        m_ref[...] = jnp.full(m_ref.shape, NEG_BIG, F32)
        l_ref[...] = jnp.zeros(l_ref.shape, F32)
        acc_ref[...] = jnp.zeros(acc_ref.shape, F32)
        q8 = q_ref[...]
        row = lax.broadcasted_iota(jnp.int32, q8.shape, 0)
        lane = lax.broadcasted_iota(jnp.int32, q8.shape, 1)
        own_map = ((lane % V_DIM) < QK_DIM) == (row < dec_seq)
        for h in range(ATT_HEADS):
            keep = own_map & ((lane // V_DIM) == h)
            qblk_ref[h * SUBLANES:(h + 1) * SUBLANES, :] = jnp.where(keep, q8, 0.0).astype(BF16)

    qblk = qblk_ref[...]
    first_of_pair = lax.broadcasted_iota(jnp.int32, (BF16_ROWS, V_DIM), 0) < SUBLANES

    def process(s, value_fn):
        m_prev = m_ref[...]
        m_new = jnp.maximum(m_prev, jnp.max(s, axis=-1, keepdims=True))
        alpha = jnp.exp2(m_prev - m_new)
        p = jnp.exp2(s - m_new)
        l_ref[...] = alpha * l_ref[...] + jnp.sum(p, axis=-1, keepdims=True)
        p = p.astype(BF16)
        pv_parts = []
        for pair in range(ATT_HEADS // 2):
            p_pair = p[pair * BF16_ROWS:(pair + 1) * BF16_ROWS]
            halves = [jnp.dot(p_pair, value_fn(h).astype(BF16), preferred_element_type=F32)
                      for h in (2 * pair, 2 * pair + 1)]
            pv_parts.append(jnp.where(first_of_pair, halves[0], halves[1]))
        acc_ref[...] = alpha * acc_ref[...] + jnp.concatenate(pv_parts, axis=0)
        m_ref[...] = m_new

    s_pages = jnp.concatenate([jnp.dot(qblk, kp[...].astype(BF16), preferred_element_type=F32)
                               for kp in k_refs], axis=1)
    process(s_pages, lambda h: jnp.concatenate(
        [vp[pl.ds(h, page, stride=ATT_HEADS), :] for vp in v_refs], axis=0))

    @pl.when(step == n_pages // group - 1)
    def _():
        kbuf_ref[...] = jnp.zeros(kbuf_ref.shape, F32)
        vbuf_ref[...] = jnp.zeros(vbuf_ref.shape, F32)
        kbuf_ref[0:SUBLANES, :] = kn_ref[...]
        vbuf_ref[0:SUBLANES, :] = vn_ref[...]
        s_new = lax.dot_general(qblk, kbuf_ref[...].astype(BF16), (((1,), (1,)), ((), ())),
                                preferred_element_type=F32)
        qi = lax.broadcasted_iota(jnp.int32, s_new.shape, 0) % dec_seq
        kj = lax.broadcasted_iota(jnp.int32, s_new.shape, 1)
        process(jnp.where(kj <= qi, s_new, NEG_BIG), lambda h: vbuf_ref[:, h * V_DIM:(h + 1) * V_DIM])

        t = acc_ref[...] / l_ref[...]
        lam = _lambda_value(lq1_ref, lk1_ref, lq2_ref, lk2_ref)
        g = g_ref[...]
        for h in range(ATT_HEADS):
            cols = slice(h * V_DIM, (h + 1) * V_DIM)
            th = t[h * SUBLANES:(h + 1) * SUBLANES]
            o = th - lam * pltpu.roll(th, SUBLANES - dec_seq, 0)
            o_ref[:, cols] = _sub_norm_gate(o, sub_ref[...], g[:, cols])

    @pl.when((pl.program_id(0) * (n_pages // group) + step) % OUTPROJ_EVERY == 0)
    def _():
        _outproj_body(*outproj_in, y_ref)


def _decode_attn(page_table, q8, cache_kt, cache_vr, k_new8, v_new8, g8, sub_w, lams, dec_seq, group,
                 outproj_rows, outproj_w):
    batch, n_pages = page_table.shape
    page = cache_kt.shape[2]
    assert page == LANES and 2 * dec_seq == SUBLANES and n_pages % group == 0
    steps = n_pages // group
    rows = outproj_rows[2].shape[0]
    n_tiles = batch * steps // OUTPROJ_EVERY
    tm = rows // n_tiles
    assert tm * n_tiles == rows and n_tiles * OUTPROJ_EVERY == batch * steps and tm % BF16_ROWS == 0
    row_tile = lambda a: pl.BlockSpec((tm, a.shape[1]), lambda b, p, pt: ((b * steps + p) // OUTPROJ_EVERY, 0))
    weight = lambda a: pl.BlockSpec(a.shape, lambda b, p, pt: (0, 0), pipeline_mode=pl.Buffered(1))
    outproj_ws = _outproj_weights(outproj_w)
    per_b = lambda a: pl.BlockSpec((None,) + a.shape[1:], lambda b, p, pt: (b, 0, 0))
    paged = lambda a, n: pl.BlockSpec((None,) + a.shape[1:],
                                      lambda b, p, pt: (pt[b, p * group + n], 0, 0))
    small = lambda a: pl.BlockSpec(a.shape, lambda b, p, pt: (0, 0))
    grid_spec = pltpu.PrefetchScalarGridSpec(
        num_scalar_prefetch=1,
        grid=(batch, n_pages // group),
        in_specs=[per_b(q8)] + [paged(cache_kt, n) for n in range(group)]
                 + [paged(cache_vr, n) for n in range(group)]
                 + [per_b(k_new8), per_b(v_new8), per_b(g8), small(sub_w)]
                 + [small(a) for a in lams]
                 + [row_tile(a) for a in outproj_rows] + [weight(a) for a in outproj_ws],
        out_specs=[per_b(g8), row_tile(outproj_rows[2])],
        scratch_shapes=[pltpu.VMEM((XROWS, 1), F32), pltpu.VMEM((XROWS, 1), F32),
                        pltpu.VMEM((XROWS, V_DIM), F32), pltpu.VMEM((XROWS, ATT_WIDTH), BF16),
                        pltpu.VMEM((page, ATT_WIDTH), F32), pltpu.VMEM((page, ATT_WIDTH), F32)],
    )
    return pl.pallas_call(
        functools.partial(_decode_attn_body, n_pages=n_pages, dec_seq=dec_seq, group=group),
        grid_spec=grid_spec,
        out_shape=(jax.ShapeDtypeStruct(g8.shape, F32), jax.ShapeDtypeStruct(outproj_rows[2].shape, F32)),
        compiler_params=_cparams(2),
        name="decode_attn_outproj",
    )(page_table, q8, *([cache_kt] * group), *([cache_vr] * group), k_new8, v_new8, g8, sub_w, *lams,
      *outproj_rows, *outproj_ws)


HEADS_PER_GROUP = SSM_HEADS // SSM_GROUPS
GROUP_WIDTH = SSM_WIDTH // SSM_GROUPS
TAIL = SUBLANES


def _pair_expand(a, j):
    lane = lax.broadcasted_iota(jnp.int32, (a.shape[0], LANES), 1)
    return jnp.where(lane < SSM_HEAD_DIM, a[:, 2 * j:2 * j + 1], a[:, 2 * j + 1:2 * j + 2])


def _ssd_body(xbc_ref, dt_ref, z_ref, cprev_ref, h0_ref, cw_ref, cb_ref, dtb_ref, alog_ref,
              dskip_ref, gn_ref,
              y_ref, cnew_ref, hnew_ref,
              xs_ref, dtp_ref, ht_ref, *, valid, n_chunks):
    q = CHUNK
    c = pl.program_id(1)
    first_step = (pl.program_id(0) == 0) & (c == 0)

    @pl.when(first_step)
    def _():
        xs_ref[...] = jnp.zeros(xs_ref.shape, F32)
        dtp_ref[...] = jnp.zeros(dtp_ref.shape, F32)

    @pl.when(c == 0)
    def _():
        xs_ref[TAIL - (CONV_W - 1):TAIL, :] = cprev_ref[...]
        ht_ref[...] = jnp.transpose(h0_ref[...])

    xs_ref[TAIL:TAIL + valid, :] = xbc_ref[...]
    dtp_ref[0:valid, :] = dt_ref[...]

    u = cb_ref[...]
    for w in range(CONV_W):
        u = u + xs_ref[pl.ds(TAIL - (CONV_W - 1) + w, q), :] * cw_ref[w:w + 1, :]
    u = _silu(u)

    @pl.when(c == n_chunks - 1)
    def _():
        cnew_ref[...] = xs_ref[pl.ds(TAIL + valid - (CONV_W - 1), CONV_W - 1), :]

    if n_chunks > 1:
        xs_ref[0:TAIL, :] = xs_ref[q:q + TAIL, :]

    xs = u[:, :SSM_WIDTH]
    bm = u[:, SSM_WIDTH:SSM_WIDTH + SSM_GROUPS * D_STATE]
    cm = u[:, SSM_WIDTH + SSM_GROUPS * D_STATE:]

    row = lax.broadcasted_iota(jnp.int32, (q, LANES), 0)
    x_dt = dtp_ref[...] + dtb_ref[...]
    dt = jnp.maximum(x_dt, 0.0) + jnp.log1p(jnp.exp(-jnp.abs(x_dt)))
    if valid < q:
        dt = jnp.where(row < valid, dt, 0.0)
    a = -jnp.exp(alog_ref[...])
    da = dt * a

    tt = lax.broadcasted_iota(jnp.int32, (q, q), 0)
    ss = lax.broadcasted_iota(jnp.int32, (q, q), 1)
    causal = ss <= tt
    tril = jnp.where(causal, 1.0, 0.0).astype(BF16)
    da_hi = da.astype(BF16)
    r1 = da - da_hi.astype(F32)
    da_mid = r1.astype(BF16)
    da_lo = (r1 - da_mid.astype(F32)).astype(BF16)
    cs3 = jnp.dot(tril, jnp.concatenate([da_hi, da_mid, da_lo], axis=1), preferred_element_type=F32)
    cs = cs3[:, :LANES] + cs3[:, LANES:2 * LANES] + cs3[:, 2 * LANES:]

    cs_t = jnp.transpose(cs)
    dt_t = jnp.transpose(dt)
    cs_last = cs[q - 1:q, :]
    e_cs = jnp.exp(cs)
    dec_end = jnp.exp(cs_last - cs) * dt
    chunk_decay = jnp.exp(cs_last)

    ht = ht_ref[...]
    ht_b = ht.astype(BF16)
    y_parts = []
    st_parts = []
    for grp in range(SSM_GROUPS):
        bg = bm[:, grp * D_STATE:(grp + 1) * D_STATE]
        cg = cm[:, grp * D_STATE:(grp + 1) * D_STATE].astype(BF16)
        cb = lax.dot_general(cg, bg.astype(BF16), (((1,), (1,)), ((), ())),
                             preferred_element_type=F32)
        y_off = jnp.dot(cg, ht_b[:, grp * GROUP_WIDTH:(grp + 1) * GROUP_WIDTH],
                        preferred_element_type=F32)
        xd_parts = []
        for jp in range(HEADS_PER_GROUP // 2):
            j = grp * (HEADS_PER_GROUP // 2) + jp
            x_pair = xs[:, j * LANES:(j + 1) * LANES]
            x_pair_b = x_pair.astype(BF16)
            y_pair = []
            for h in (2 * j, 2 * j + 1):
                diff = cs[:, h:h + 1] - cs_t[h:h + 1, :]
                lmat = jnp.exp(jnp.where(causal, diff, NEG_BIG))
                wmat = (cb * lmat * dt_t[h:h + 1, :]).astype(BF16)
                y_pair.append(jnp.dot(wmat, x_pair_b, preferred_element_type=F32))
            lane = lax.broadcasted_iota(jnp.int32, (q, LANES), 1)
            y_diag = jnp.where(lane < SSM_HEAD_DIM, y_pair[0], y_pair[1])
            y_parts.append(y_diag + y_off[:, jp * LANES:(jp + 1) * LANES] * _pair_expand(e_cs, j))
            xd_parts.append((x_pair * _pair_expand(dec_end, j)).astype(BF16))
        xd = jnp.concatenate(xd_parts, axis=1)
        st_parts.append(jnp.dot(jnp.transpose(bg).astype(BF16), xd, preferred_element_type=F32))
    decay_lanes = jnp.concatenate([_pair_expand(chunk_decay, j) for j in range(SSM_HEADS // 2)], axis=1)
    ht_new = ht * decay_lanes + jnp.concatenate(st_parts, axis=1)
    ht_ref[...] = ht_new

    @pl.when(c == n_chunks - 1)
    def _():
        hnew_ref[...] = jnp.transpose(ht_new)

    y = jnp.concatenate(y_parts, axis=1) + dskip_ref[...] * xs
    y = y[0:valid] * _silu(z_ref[...])
    outs = []
    for grp in range(SSM_GROUPS):
        yg = y[:, grp * GROUP_WIDTH:(grp + 1) * GROUP_WIDTH]
        outs.append(yg * lax.rsqrt(jnp.mean(yg * yg, axis=-1, keepdims=True) + EPS))
    y_ref[...] = (jnp.concatenate(outs, axis=1) * gn_ref[...]).astype(y_ref.dtype)


def _ssd(xbc, dt_raw, z, conv_prev, h0, w):
    n_seq, rows, _ = xbc.shape
    valid = min(rows, CHUNK)
    n_chunks = rows // valid
    assert n_chunks * valid == rows and valid >= CONV_W - 1
    blk = lambda n: pl.BlockSpec((None, valid, n), lambda s, c: (s, c, 0))
    per_seq = lambda a: pl.BlockSpec((None,) + a.shape[1:], lambda s, c: (s, 0, 0))
    small = lambda a: pl.BlockSpec(a.shape, lambda s, c: (0, 0))
    consts = [w["conv_w"], w["conv_b"], w["dt_bias"], w["a_log"], w["d_skip"], w["gnorm"]]
    return pl.pallas_call(
        functools.partial(_ssd_body, valid=valid, n_chunks=n_chunks),
        grid=(n_seq, n_chunks),
        in_specs=[blk(CONV_DIM), blk(LANES), blk(SSM_WIDTH), per_seq(conv_prev), per_seq(h0)]
                 + [small(a) for a in consts],
        out_specs=[blk(SSM_WIDTH), per_seq(conv_prev), per_seq(h0)],
        out_shape=(jax.ShapeDtypeStruct((n_seq, rows, SSM_WIDTH), BF16),
                   jax.ShapeDtypeStruct(conv_prev.shape, F32),
                   jax.ShapeDtypeStruct(h0.shape, F32)),
        scratch_shapes=[pltpu.VMEM((CHUNK + TAIL, CONV_DIM), F32),
                        pltpu.VMEM((CHUNK, LANES), F32),
                        pltpu.VMEM((D_STATE, SSM_WIDTH), F32)],
        compiler_params=_cparams(2),
        name="ssd",
    )(xbc, dt_raw, z, conv_prev, h0, *consts)


def _outproj_body(att_ref, ssm_ref, x_ref, p_ref, woa_ref, wos_ref, wpp_ref, wpg_ref, fn_ref, y_ref):
    hmid = (x_ref[...]
            + jnp.dot(att_ref[...], woa_ref[...], preferred_element_type=F32)
            + jnp.dot(ssm_ref[...], wos_ref[...], preferred_element_type=F32))
    emb = jnp.dot(p_ref[...].astype(BF16), wpp_ref[...], preferred_element_type=F32)
    gate = jax.nn.sigmoid(jnp.dot(hmid.astype(BF16), wpg_ref[...], preferred_element_type=F32))
    out = hmid + emb * gate
    y_ref[...] = out * lax.rsqrt(jnp.mean(out * out, axis=-1, keepdims=True) + EPS) * fn_ref[...]


def _outproj_weights(w):
    return [w["out_att"], w["out_ssm"], w["ple_proj"], w["ple_gate"], w["final_norm"]]


def _outproj(att, ssm, x, p, w, tm):
    rows = x.shape[0]
    row = lambda n: pl.BlockSpec((tm, n), lambda i: (i, 0))
    wspec = lambda a: pl.BlockSpec(a.shape, lambda i: (0, 0))
    ws = _outproj_weights(w)
    return pl.pallas_call(
        _outproj_body,
        grid=(rows // tm,),
        in_specs=[row(ATT_WIDTH), row(SSM_WIDTH), row(D_MODEL), row(PLE_DIM)] + [wspec(a) for a in ws],
        out_specs=row(D_MODEL),
        out_shape=jax.ShapeDtypeStruct((rows, D_MODEL), F32),
        compiler_params=_cparams(1),
        name="outproj",
    )(att, ssm, x, p, *ws)


def _rope_tables(pos):
    half = QK_DIM // 2
    inv = ROPE_THETA ** (-jnp.arange(0, QK_DIM, 2, dtype=F32) / QK_DIM)
    ang = pos.astype(F32)[:, None] * inv[None, :]
    cos = jnp.tile(jnp.cos(ang), (1, LANES // half))
    sin = jnp.sin(ang)
    sin = jnp.tile(jnp.concatenate([-sin, sin], axis=1), (1, LANES // QK_DIM))
    return cos, sin


def _pad_to(a, axis, size):
    pads = [(0, 0)] * a.ndim
    pads[axis] = (0, size - a.shape[axis])
    return jnp.pad(a, pads)


def kernel(x_prompt, x_sample, cache_k, cache_v, state_conv, state_ssm, page_table, p_prompt, p_sample, w_norm, w_in, lambda_q1, lambda_k1, lambda_q2, lambda_k2, subln_w, conv_w, conv_b, dt_bias, A_log, D_skip, gnorm_w, w_out, w_ple_proj, w_ple_gate, final_norm_w):
    assert w_norm.shape[0] == 1 and x_prompt.shape[0] == 1
    seq = x_prompt.shape[1]
    dec_batch, dec_seq, _ = x_sample.shape
    n_pool, page = cache_k.shape[1], cache_k.shape[2]
    past = page_table.shape[1] * page

    w_in0 = w_in[0]
    splits = [0, 1024, 2048, 3072, 4096, 5120, 5120 + CONV_DIM, 5120 + CONV_DIM + SSM_HEADS]
    names = ["q", "k", "v", "g", "z", "xbc", "dt"]
    w = {n: w_in0[:, a:b].astype(BF16) for n, a, b in zip(names, splits[:-1], splits[1:])}
    w["dt"] = _pad_to(w["dt"], 1, LANES)
    w["norm"] = w_norm
    ssd_w = {"conv_w": conv_w[0], "conv_b": conv_b, "dt_bias": _pad_to(dt_bias, 1, LANES),
             "a_log": _pad_to(A_log, 1, LANES), "d_skip": jnp.repeat(D_skip, SSM_HEAD_DIM, axis=1),
             "gnorm": gnorm_w}
    out_w = {"out_att": w_out[0, :ATT_WIDTH].astype(BF16), "out_ssm": w_out[0, ATT_WIDTH:].astype(BF16),
             "ple_proj": w_ple_proj[0].astype(BF16), "ple_gate": w_ple_gate[0].astype(BF16),
             "final_norm": final_norm_w[None, :]}
    lams = [lambda_q1, lambda_k1, lambda_q2, lambda_k2]

    cos_p, sin_p = _rope_tables(jnp.arange(seq))
    q_p, k_p, v_p, g_p, z_p, xbc_p, dt_p, kb_p, vt_p = _inproj(
        x_prompt[0], w, cos_p, sin_p, tm=INPROJ_TM, attn_tk=ATTN_TK)
    att_p = _prompt_attn(q_p, kb_p, vt_p, g_p, subln_w, lams, tq=ATTN_TQ, tk=ATTN_TK)
    ssm_p, conv_p, h_p = _ssd(xbc_p[None], dt_p[None], z_p[None],
                              jnp.zeros((1, CONV_W - 1, CONV_DIM), F32),
                              jnp.zeros((1, SSM_WIDTH, D_STATE), F32), ssd_w)

    n_rows = dec_batch * dec_seq
    cos_s, sin_s = _rope_tables(past + jnp.arange(n_rows) % dec_seq)
    q_s, k_s, v_s, g_s, z_s, xbc_s, dt_s = _inproj(x_sample.reshape(n_rows, D_MODEL), w,
                                                 cos_s, sin_s, tm=n_rows)
    per_seq = lambda a: a.reshape(dec_batch, dec_seq, a.shape[-1])
    q3 = per_seq(q_s)
    cache_kt = jnp.transpose(cache_k[0], (0, 2, 3, 4, 1)).reshape(n_pool, ATT_WIDTH, page)
    cache_vr = cache_v[0].reshape(n_pool, page * ATT_HEADS, V_DIM)
    pad8 = lambda a: _pad_to(per_seq(a), 1, SUBLANES)
    att_s, y_p = _decode_attn(page_table, jnp.concatenate([q3, q3], axis=1).astype(F32), cache_kt, cache_vr,
                              pad8(k_s), pad8(v_s), pad8(g_s), subln_w, lams, dec_seq, DECODE_PAGES,
                              (att_p, ssm_p[0], x_prompt[0], p_prompt[0, 0]), out_w)
    att_s = att_s[:, :dec_seq].reshape(n_rows, ATT_WIDTH).astype(BF16)
    ssm_s, conv_s, h_s = _ssd(per_seq(xbc_s), per_seq(dt_s), per_seq(z_s), state_conv[0],
                              state_ssm[0].reshape(dec_batch, SSM_WIDTH, D_STATE), ssd_w)
    y_s = _outproj(att_s, ssm_s.reshape(n_rows, SSM_WIDTH), x_sample.reshape(n_rows, D_MODEL),
                   p_sample[0].reshape(n_rows, PLE_DIM), out_w, tm=n_rows)

    hp = (SSM_HEADS, SSM_HEAD_DIM, D_STATE)
    return (y_p[None],
            y_s.reshape(dec_batch, dec_seq, D_MODEL),
            k_p.reshape(1, 1, seq, ATT_HEADS, 2, QK_DIM),
            v_p.reshape(1, 1, seq, ATT_HEADS, V_DIM),
            conv_p[None],
            h_p.reshape((1, 1) + hp),
            k_s.reshape(1, dec_batch, dec_seq, ATT_HEADS, 2, QK_DIM),
            v_s.reshape(1, dec_batch, dec_seq, ATT_HEADS, V_DIM),
            conv_s[None],
            h_s.reshape((1, dec_batch) + hp))
```
